```python
import jax, jax.numpy as jnp
from jax import lax
import numpy as np

D_MODEL = 1024
BATCH = 8
SEQ = 2048
DEPTH = 4
DEC_BATCH = 128
DEC_SEQ = 8
PAST_LEN = 16384
PAGE_SIZE = 128

N_MIXERS = 2
N_LRU = (DEPTH + 1) // 2
N_RWKV = DEPTH // 2
D_RNN = D_MODEL
LRU_BLOCKS = 4
LRU_BLOCK_W = D_RNN // LRU_BLOCKS
CONV_W = 4
RG_C = 8.0
HEAD_SIZE = 64
RWKV_HEADS = D_MODEL // HEAD_SIZE
D_DECAY_LORA = 64
D_AAA_LORA = 64
D_MV_LORA = 32
D_GATE_LORA = 160
D_FF = 2816
FFN_RES = 0.5
N_MOD = 9
NORM_EPS = 1e-6
GN_EPS = HEAD_SIZE * 1e-5

kernel_name = 'hybrid_rglru_rwkv7_adaln_macaron_step'


def rms_norm(x):
    xf = x.astype(jnp.float32)
    return (xf * lax.rsqrt(jnp.mean(xf * xf, axis=-1, keepdims=True) + NORM_EPS)).astype(x.dtype)


def modulate(x, shift, scale):
    return rms_norm(x) * (1.0 + scale) + shift


def swiglu(h, w_in, w_out):
    gate, up = jnp.split(h @ w_in, 2, axis=-1)
    return (jax.nn.silu(gate) * up) @ w_out


def causal_conv(x, buf, w, b):
    T = x.shape[1]
    xp = jnp.concatenate([buf.astype(x.dtype), x], axis=1)
    y = b
    for j in range(CONV_W):
        y = y + xp[:, j:j + T] * w[j]
    return y, xp[:, T:]


def linear_scan(a, u, h0):
    def comb(l, r):
        return l[0] * r[0], r[0] * l[1] + r[1]
    a_cum, u_cum = lax.associative_scan(comb, (a, u), axis=1)
    hs = a_cum * h0[:, None, :] + u_cum
    return hs, hs[:, -1]


def rglru_block(h, h0, conv_buf, w_in, conv_w, conv_b, gate_w, gate_b, lam, w_out):
    B, T, _ = h.shape
    gate_branch, rec = jnp.split(h @ w_in, 2, axis=-1)
    xc, new_buf = causal_conv(rec, conv_buf, conv_w, conv_b)
    xf = xc.astype(jnp.float32)
    gates = jnp.einsum('btnc,ncg->btng', xf.reshape(B, T, LRU_BLOCKS, LRU_BLOCK_W), gate_w) + gate_b
    r = jax.nn.sigmoid(gates[..., :LRU_BLOCK_W]).reshape(B, T, D_RNN)
    i = jax.nn.sigmoid(gates[..., LRU_BLOCK_W:]).reshape(B, T, D_RNN)
    log_a = RG_C * r * jax.nn.log_sigmoid(lam.astype(jnp.float32))
    a = jnp.exp(log_a)
    u = jnp.sqrt(-jnp.expm1(2.0 * log_a)) * (i * xf)
    hs, h_last = linear_scan(a, u, h0.astype(jnp.float32))
    y = (hs.astype(h.dtype) * jax.nn.gelu(gate_branch)) @ w_out
    return y, h_last, new_buf


def wkv_scan(r, w, k, v, a, b, S0):
    def step(S, inp):
        r_t, w_t, k_t, v_t, a_t, b_t = inp
        sa = jnp.einsum('bhvk,bhk->bhv', S, a_t)
        S = S * w_t[:, :, None, :] + sa[..., None] * b_t[:, :, None, :] + v_t[..., None] * k_t[:, :, None, :]
        return S, jnp.einsum('bhvk,bhk->bhv', S, r_t)
    seq = tuple(jnp.moveaxis(t, 1, 0) for t in (r, w, k, v, a, b))
    S_T, ys = lax.scan(step, S0, seq)
    return jnp.moveaxis(ys, 0, 1), S_T


def rwkv7_time_mix(h, shift0, S0, mu, w_rkv, w_o, w0, w1, w2, a0, a1, a2, g1, g2,
                   k_k, k_a, r_k, ln_w, ln_b, v_first, vres):
    B, T, D = h.shape
    xf = h.astype(jnp.float32)
    x_prev = jnp.concatenate([shift0.astype(jnp.float32)[:, None, :], xf[:, :-1]], axis=1)
    xm = xf[None] + (x_prev - xf)[None] * mu[:, None, None, :]
    r, k, v = jnp.einsum('jbtd,jde->jbte', xm[:3], w_rkv)
    w_log = -jax.nn.softplus(-(w0 + jnp.tanh(xm[3] @ w1) @ w2)) - 0.5
    decay = jnp.exp(-jnp.exp(w_log))
    if vres is not None:
        v0, v1, v2 = vres
        v = v + (v_first - v) * jax.nn.sigmoid(v0 + (xm[2] @ v1) @ v2)
    a = jax.nn.sigmoid(a0 + (xm[4] @ a1) @ a2)
    g = jax.nn.sigmoid(xm[5] @ g1) @ g2
    kk = (k * k_k).reshape(B, T, RWKV_HEADS, HEAD_SIZE)
    kk = kk / jnp.maximum(jnp.sqrt(jnp.sum(kk * kk, axis=-1, keepdims=True)), 1e-12)
    k = k * (1.0 + (a - 1.0) * k_a)
    rh, wh, kh, vh, ah = (t.reshape(B, T, RWKV_HEADS, HEAD_SIZE) for t in (r, decay, k, v, a))
    ys, S_T = wkv_scan(rh, wh, kh, vh, -kk, kk * ah, S0.astype(jnp.float32))
    mean = jnp.mean(ys, axis=-1, keepdims=True)
    var = jnp.mean(jnp.square(ys - mean), axis=-1, keepdims=True)
    yn = ((ys - mean) * lax.rsqrt(var + GN_EPS)).reshape(B, T, D) * ln_w + ln_b
    bonus = jnp.sum(rh * kh * r_k, axis=-1, keepdims=True) * vh
    out = ((yn + bonus.reshape(B, T, D)) * g) @ w_o
    return out.astype(h.dtype), xf[:, -1], S_T, v


def trunk(x, c, lru_h0, lru_conv0, rwkv_shift0, rwkv_wkv0, P):
    B = x.shape[0]
    new_h, new_conv, new_shift, new_wkv = [], [], [], []
    v_first = None
    for layer in range(DEPTH):
        j = layer // N_MIXERS
        mod = (jax.nn.silu(c) @ P['ada_w'][layer] + P['ada_b'][layer]).reshape(B, N_MOD, D_MODEL)
        sh1, sc1, g1, sh2, sc2, g2, sh3, sc3, g3 = [mod[:, m, None, :] for m in range(N_MOD)]
        x = x + FFN_RES * (1.0 + g1) * swiglu(modulate(x, sh1, sc1), P['ffn_w_in'][layer, 0], P['ffn_w_out'][layer, 0])
        h = modulate(x, sh2, sc2)
        if layer % N_MIXERS == 0:
            y, h_last, buf = rglru_block(h, lru_h0[j], lru_conv0[j], P['lru_w_in'][j], P['lru_conv_w'][j],
                                         P['lru_conv_b'][j], P['lru_gate_w'][j], P['lru_gate_b'][j],
                                         P['lru_lambda'][j], P['lru_w_out'][j])
            new_h.append(h_last)
            new_conv.append(buf)
        else:
            vres = None if j == 0 else (P['rwkv_v0'][j - 1], P['rwkv_v1'][j - 1], P['rwkv_v2'][j - 1])
            y, shift_last, S_T, v = rwkv7_time_mix(
                h, rwkv_shift0[j], rwkv_wkv0[j], P['rwkv_mu'][j], P['rwkv_w_rkv'][j], P['rwkv_w_o'][j],
                P['rwkv_w0'][j], P['rwkv_w1'][j], P['rwkv_w2'][j], P['rwkv_a0'][j], P['rwkv_a1'][j],
                P['rwkv_a2'][j], P['rwkv_g1'][j], P['rwkv_g2'][j], P['rwkv_k_k'][j], P['rwkv_k_a'][j],
                P['rwkv_r_k'][j], P['rwkv_ln_w'][j], P['rwkv_ln_b'][j], v_first, vres)
            if v_first is None:
                v_first = v
            new_shift.append(shift_last)
            new_wkv.append(S_T)
        x = x + (1.0 + g2) * y
        x = x + FFN_RES * (1.0 + g3) * swiglu(modulate(x, sh3, sc3), P['ffn_w_in'][layer, 1], P['ffn_w_out'][layer, 1])
    y = rms_norm(x) * P['final_gain']
    return y, jnp.stack(new_h), jnp.stack(new_conv), jnp.stack(new_shift), jnp.stack(new_wkv)


def setup_inputs(seed: int = 0) -> dict:
    key = jax.random.key(seed)
    ks = iter(jax.random.split(key, 48))

    def nrm(shape, scale):
        return jax.random.normal(next(ks), shape, jnp.float32) * scale

    def unif(shape, lo, hi):
        return jax.random.uniform(next(ks), shape, jnp.float32, lo, hi)

    D = D_MODEL
    x_prompt = nrm((BATCH, SEQ, D), 1.0)
    x_sample = nrm((DEC_BATCH, DEC_SEQ, D), 1.0)
    state_lru_h = nrm((N_LRU, DEC_BATCH, D_RNN), 0.5)
    state_lru_conv = nrm((N_LRU, DEC_BATCH, CONV_W - 1, D_RNN), 1.0)
    state_rwkv_shift = nrm((N_RWKV, DEC_BATCH, D), 1.0)
    state_rwkv_wkv = nrm((N_RWKV, DEC_BATCH, RWKV_HEADS, HEAD_SIZE, HEAD_SIZE), 1.0)
    c_prompt = nrm((BATCH, D), 1.0)
    c_sample = nrm((DEC_BATCH, D), 1.0)
    ada_w = nrm((DEPTH, D, N_MOD * D), 0.2 * D ** -0.5)
    ada_b = nrm((DEPTH, N_MOD * D), 0.02)
    ffn_w_in = nrm((DEPTH, 2, D, 2 * D_FF), D ** -0.5)
    ffn_w_out = nrm((DEPTH, 2, D_FF, D), D_FF ** -0.5)
    lru_w_in = nrm((N_LRU, D, 2 * D_RNN), D ** -0.5)
    lru_conv_w = nrm((N_LRU, CONV_W, D_RNN), CONV_W ** -0.5)
    lru_conv_b = nrm((N_LRU, D_RNN), 0.02)
    lru_gate_w = nrm((N_LRU, LRU_BLOCKS, LRU_BLOCK_W, 2 * LRU_BLOCK_W), LRU_BLOCK_W ** -0.5)
    lru_gate_b = nrm((N_LRU, LRU_BLOCKS, 2 * LRU_BLOCK_W), 0.02)
    a_target = unif((N_LRU, D_RNN), 0.9, 0.999)
    p = a_target ** (1.0 / RG_C)
    lru_lambda = jnp.log(p) - jnp.log1p(-p)
    lru_w_out = nrm((N_LRU, D_RNN, D), D_RNN ** -0.5)
    rwkv_mu = unif((N_RWKV, 6, D), 0.0, 1.0)
    rwkv_w_rkv = nrm((N_RWKV, 3, D, D), D ** -0.5)
    rwkv_w_o = nrm((N_RWKV, D, D), D ** -0.5)
    rwkv_w0 = unif((N_RWKV, D), -6.0, 1.0)
    rwkv_w1 = nrm((N_RWKV, D, D_DECAY_LORA), D ** -0.5)
    rwkv_w2 = nrm((N_RWKV, D_DECAY_LORA, D), 0.5 * D_DECAY_LORA ** -0.5)
    rwkv_a0 = nrm((N_RWKV, D), 0.1)
    rwkv_a1 = nrm((N_RWKV, D, D_AAA_LORA), D ** -0.5)
    rwkv_a2 = nrm((N_RWKV, D_AAA_LORA, D), 0.5 * D_AAA_LORA ** -0.5)
    rwkv_v0 = nrm((N_RWKV - 1, D), 0.1)
    rwkv_v1 = nrm((N_RWKV - 1, D, D_MV_LORA), D ** -0.5)
    rwkv_v2 = nrm((N_RWKV - 1, D_MV_LORA, D), 0.5 * D_MV_LORA ** -0.5)
    rwkv_g1 = nrm((N_RWKV, D, D_GATE_LORA), D ** -0.5)
    rwkv_g2 = nrm((N_RWKV, D_GATE_LORA, D), D_GATE_LORA ** -0.5)
    rwkv_k_k = 0.85 + nrm((N_RWKV, D), 0.05)
    rwkv_k_a = 1.0 + nrm((N_RWKV, D), 0.05)
    rwkv_r_k = nrm((N_RWKV, RWKV_HEADS, HEAD_SIZE), 0.1)
    rwkv_ln_w = 1.0 + nrm((N_RWKV, D), 0.05)
    rwkv_ln_b = nrm((N_RWKV, D), 0.02)
    final_gain = 1.0 + nrm((D,), 0.02)
    return {'x_prompt': x_prompt, 'x_sample': x_sample,
            'state_lru_h': state_lru_h, 'state_lru_conv': state_lru_conv,
            'state_rwkv_shift': state_rwkv_shift, 'state_rwkv_wkv': state_rwkv_wkv,
            'c_prompt': c_prompt, 'c_sample': c_sample,
            'ada_w': ada_w, 'ada_b': ada_b, 'ffn_w_in': ffn_w_in, 'ffn_w_out': ffn_w_out,
            'lru_w_in': lru_w_in, 'lru_conv_w': lru_conv_w, 'lru_conv_b': lru_conv_b,
            'lru_gate_w': lru_gate_w, 'lru_gate_b': lru_gate_b, 'lru_lambda': lru_lambda, 'lru_w_out': lru_w_out,
            'rwkv_mu': rwkv_mu, 'rwkv_w_rkv': rwkv_w_rkv, 'rwkv_w_o': rwkv_w_o,
            'rwkv_w0': rwkv_w0, 'rwkv_w1': rwkv_w1, 'rwkv_w2': rwkv_w2,
            'rwkv_a0': rwkv_a0, 'rwkv_a1': rwkv_a1, 'rwkv_a2': rwkv_a2,
            'rwkv_v0': rwkv_v0, 'rwkv_v1': rwkv_v1, 'rwkv_v2': rwkv_v2,
            'rwkv_g1': rwkv_g1, 'rwkv_g2': rwkv_g2, 'rwkv_k_k': rwkv_k_k, 'rwkv_k_a': rwkv_k_a,
            'rwkv_r_k': rwkv_r_k, 'rwkv_ln_w': rwkv_ln_w, 'rwkv_ln_b': rwkv_ln_b,
            'final_gain': final_gain}


def reference(x_prompt, x_sample, state_lru_h, state_lru_conv, state_rwkv_shift, state_rwkv_wkv,
              c_prompt, c_sample, ada_w, ada_b, ffn_w_in, ffn_w_out,
              lru_w_in, lru_conv_w, lru_conv_b, lru_gate_w, lru_gate_b, lru_lambda, lru_w_out,
              rwkv_mu, rwkv_w_rkv, rwkv_w_o, rwkv_w0, rwkv_w1, rwkv_w2, rwkv_a0, rwkv_a1, rwkv_a2,
              rwkv_v0, rwkv_v1, rwkv_v2, rwkv_g1, rwkv_g2, rwkv_k_k, rwkv_k_a, rwkv_r_k,
              rwkv_ln_w, rwkv_ln_b, final_gain):
    P = dict(ada_w=ada_w, ada_b=ada_b, ffn_w_in=ffn_w_in, ffn_w_out=ffn_w_out,
             lru_w_in=lru_w_in, lru_conv_w=lru_conv_w, lru_conv_b=lru_conv_b, lru_gate_w=lru_gate_w,
             lru_gate_b=lru_gate_b, lru_lambda=lru_lambda, lru_w_out=lru_w_out,
             rwkv_mu=rwkv_mu, rwkv_w_rkv=rwkv_w_rkv, rwkv_w_o=rwkv_w_o, rwkv_w0=rwkv_w0, rwkv_w1=rwkv_w1,
             rwkv_w2=rwkv_w2, rwkv_a0=rwkv_a0, rwkv_a1=rwkv_a1, rwkv_a2=rwkv_a2, rwkv_v0=rwkv_v0,
             rwkv_v1=rwkv_v1, rwkv_v2=rwkv_v2, rwkv_g1=rwkv_g1, rwkv_g2=rwkv_g2, rwkv_k_k=rwkv_k_k,
             rwkv_k_a=rwkv_k_a, rwkv_r_k=rwkv_r_k, rwkv_ln_w=rwkv_ln_w, rwkv_ln_b=rwkv_ln_b,
             final_gain=final_gain)
    B = x_prompt.shape[0]
    h0 = jnp.zeros((N_LRU, B, D_RNN), jnp.float32)
    conv0 = jnp.zeros((N_LRU, B, CONV_W - 1, D_RNN), jnp.float32)
    shift0 = jnp.zeros((N_RWKV, B, D_MODEL), jnp.float32)
    wkv0 = jnp.zeros((N_RWKV, B, RWKV_HEADS, HEAD_SIZE, HEAD_SIZE), jnp.float32)
    y_prompt, p_lru_h, p_lru_conv, p_rwkv_shift, p_rwkv_wkv = trunk(x_prompt, c_prompt, h0, conv0, shift0, wkv0, P)
    y_sample, s_lru_h, s_lru_conv, s_rwkv_shift, s_rwkv_wkv = trunk(
        x_sample, c_sample, state_lru_h, state_lru_conv, state_rwkv_shift, state_rwkv_wkv, P)
    return (y_prompt, y_sample, p_lru_h, p_lru_conv, p_rwkv_shift, p_rwkv_wkv,
            s_lru_h, s_lru_conv, s_rwkv_shift, s_rwkv_wkv)
```

```python
import functools

import jax
import jax.numpy as jnp
from jax import lax
from jax.experimental import pallas as pl
from jax.experimental.pallas import tpu as pltpu

D_MODEL = 1024
DEPTH = 4
D_RNN = D_MODEL
LRU_BLOCKS = 4
LRU_BLOCK_W = D_RNN // LRU_BLOCKS
CONV_W = 4
RG_C = 8.0
HEAD_SIZE = 64
RWKV_HEADS = D_MODEL // HEAD_SIZE
D_FF = 2816
FFN_RES = 0.5
N_MOD = 9
NORM_EPS = 1e-6
GN_EPS = HEAD_SIZE * 1e-5

LANES = 128
SUBLANES = 8
HEAD_PAIRS = D_MODEL // LANES
LORA_PAD = 128
VMEM_LIMIT_BYTES = 56 * 1024 * 1024
ROW_TILE = 512
RWKV_IN_ROW_TILE = 256
FFN_CHUNK = 256
ADA_TILE = 1152
SCAN_CARRY_VREGS = 16

F32 = jnp.float32
BF16 = jnp.bfloat16


def _dot(a, b):
    return jnp.dot(a, b, preferred_element_type=F32)


def _params(*sem):
    return pltpu.CompilerParams(dimension_semantics=sem, vmem_limit_bytes=VMEM_LIMIT_BYTES)


def _resident(block_shape, index_map):
    return pl.BlockSpec(block_shape, index_map, pipeline_mode=pl.Buffered(1))


def _rms_mod(x, shift, scale):
    ms = jnp.mean(x * x, axis=-1, keepdims=True)
    return (x * lax.rsqrt(ms + NORM_EPS)) * (1.0 + scale)[None] + shift[None]


def _softplus(x):
    return jnp.maximum(x, 0.0) + jnp.log1p(jnp.exp(-jnp.abs(x)))


def _head_ones():
    r = lax.broadcasted_iota(jnp.int32, (LANES, LANES), 0) // HEAD_SIZE
    c = lax.broadcasted_iota(jnp.int32, (LANES, LANES), 1) // HEAD_SIZE
    return (r == c).astype(BF16)


def _head_sum(x, ones):
    outs = []
    for c in range(HEAD_PAIRS):
        xc = x[:, c * LANES:(c + 1) * LANES]
        hi = xc.astype(BF16)
        lo = (xc - hi.astype(F32)).astype(BF16)
        outs.append(_dot(hi, ones) + _dot(lo, ones))
    return jnp.concatenate(outs, axis=-1)


def _ada_kernel(c_ref, w_ref, b_ref, o_ref):
    c = c_ref[...]
    s = (c * jax.nn.sigmoid(c)).astype(BF16)
    o_ref[...] = _dot(s, w_ref[...].astype(BF16)) + b_ref[...]


def _ada(c_all, ada_w, ada_b):
    n = c_all.shape[0]
    nd = N_MOD * D_MODEL
    return pl.pallas_call(
        _ada_kernel,
        grid=(DEPTH, nd // ADA_TILE),
        in_specs=[
            pl.BlockSpec((n, D_MODEL), lambda l, j: (0, 0)),
            pl.BlockSpec((None, D_MODEL, ADA_TILE), lambda l, j: (l, 0, j)),
            pl.BlockSpec((None, 1, ADA_TILE), lambda l, j: (l, 0, j)),
        ],
        out_specs=pl.BlockSpec((None, n, ADA_TILE), lambda l, j: (l, 0, j)),
        out_shape=jax.ShapeDtypeStruct((DEPTH, n, nd), F32),
        compiler_params=_params("parallel", "parallel"),
        name="ada_mod",
    )(c_all, ada_w, ada_b.reshape(DEPTH, 1, nd))


def _mod_spec(bsz, layer, m):
    return pl.BlockSpec((None, bsz, D_MODEL), lambda *_: (layer, 0, m))


def _time_tile(t_len, bsz, rows=ROW_TILE):
    return max(1, min(t_len, rows // bsz))


def _ffn_kernel(x_ref, sh_ref, sc_ref, g_ref, win_ref, wout_ref, o_ref):
    x = x_ref[...]
    tt, bsz, d = x.shape
    h = _rms_mod(x, sh_ref[...], sc_ref[...]).reshape(tt * bsz, d).astype(BF16)
    acc = jnp.zeros((tt * bsz, d), F32)
    for j in range(D_FF // FFN_CHUNK):
        lo = j * FFN_CHUNK
        gate = _dot(h, win_ref[:, lo:lo + FFN_CHUNK])
        up = _dot(h, win_ref[:, D_FF + lo:D_FF + lo + FFN_CHUNK])
        act = (gate * jax.nn.sigmoid(gate) * up).astype(BF16)
        acc = acc + _dot(act, wout_ref[lo:lo + FFN_CHUNK, :])
    o_ref[...] = x + (FFN_RES * (1.0 + g_ref[...]))[None] * acc.reshape(tt, bsz, d)


def _ffn(x, mods, layer, sub, w_in, w_out):
    t_len, bsz, d = x.shape
    tt = _time_tile(t_len, bsz)
    m0 = 0 if sub == 0 else 6
    xspec = pl.BlockSpec((tt, bsz, d), lambda i: (i, 0, 0))
    return pl.pallas_call(
        _ffn_kernel,
        grid=(t_len // tt,),
        in_specs=[
            xspec,
            _mod_spec(bsz, layer, m0), _mod_spec(bsz, layer, m0 + 1), _mod_spec(bsz, layer, m0 + 2),
            _resident((None, None, d, 2 * D_FF), lambda i: (layer, sub, 0, 0)),
            _resident((None, None, D_FF, d), lambda i: (layer, sub, 0, 0)),
        ],
        out_specs=xspec,
        out_shape=jax.ShapeDtypeStruct(x.shape, F32),
        compiler_params=_params("parallel"),
        name="ffn",
    )(x, mods, mods, mods, w_in, w_out)


def _lru_kernel(x_ref, sh_ref, sc_ref, g_ref, h0_ref, conv0_ref, win_ref, cw_ref, cb_ref, gw_ref,
                gb_ref, lam_ref, wout_ref, o_ref, hlast_ref, buf_ref, a_scr, u_scr):
    @pl.when(pl.program_id(0) == 0)
    def _():
        hlast_ref[...] = h0_ref[...]
        buf_ref[...] = conv0_ref[...]

    x = x_ref[...]
    tt, bsz, d = x.shape
    tm = tt * bsz
    h = _rms_mod(x, sh_ref[...], sc_ref[...]).reshape(tm, d).astype(BF16)
    proj = _dot(h, win_ref[...])
    gate_branch = proj[:, :d]
    xp = jnp.concatenate([buf_ref[...], proj[:, d:].reshape(tt, bsz, d)], axis=0)
    cw = cw_ref[...]
    xc = cb_ref[...][None]
    for j in range(CONV_W):
        xc = xc + xp[j:j + tt] * cw[j:j + 1][None]
    buf_ref[...] = xp[tt:tt + CONV_W - 1]
    xc = xc.reshape(tm, d)

    lam = lam_ref[...]
    log_sig_lam = -_softplus(-lam)
    for n in range(LRU_BLOCKS):
        sl = slice(n * LRU_BLOCK_W, (n + 1) * LRU_BLOCK_W)
        xb = xc[:, sl]
        gates = _dot(xb.astype(BF16), gw_ref[n]) + gb_ref[n:n + 1, :]
        r = jax.nn.sigmoid(gates[:, :LRU_BLOCK_W])
        ig = jax.nn.sigmoid(gates[:, LRU_BLOCK_W:])
        log_a = RG_C * r * log_sig_lam[:, sl]
        a_scr[:, sl] = jnp.exp(log_a)
        th = jnp.tanh(log_a)
        u_scr[:, sl] = jnp.sqrt(-2.0 * th / (1.0 - th)) * (ig * xb)

    cwid = min(d, SCAN_CARRY_VREGS * SUBLANES * LANES // bsz)
    for c in range(d // cwid):
        ls = slice(c * cwid, (c + 1) * cwid)

        def step(t, hc, ls=ls):
            rows = pl.ds(pl.multiple_of(t * bsz, bsz), bsz)
            hc = a_scr[rows, ls] * hc + u_scr[rows, ls]
            u_scr[rows, ls] = hc
            return hc

        hlast_ref[:, ls] = lax.fori_loop(0, tt, step, hlast_ref[:, ls])

    gelu = 0.5 * gate_branch * (1.0 + jnp.tanh(
        0.7978845608028654 * (gate_branch + 0.044715 * (gate_branch * gate_branch * gate_branch))))
    y = _dot((u_scr[...] * gelu).astype(BF16), wout_ref[...])
    o_ref[...] = x + (1.0 + g_ref[...])[None] * y.reshape(tt, bsz, d)


def _lru(x, mods, layer, h0, conv0, w_in, conv_w, conv_b, gate_w, gate_b, lam, w_out):
    t_len, bsz, d = x.shape
    tt = _time_tile(t_len, bsz)
    xspec = pl.BlockSpec((tt, bsz, d), lambda i: (i, 0, 0))

    def full(a):
        return _resident(a.shape, lambda i, n=a.ndim: (0,) * n)

    args = (h0, conv0, w_in, conv_w, conv_b, gate_w, gate_b, lam, w_out)
    return pl.pallas_call(
        _lru_kernel,
        grid=(t_len // tt,),
        in_specs=[xspec, _mod_spec(bsz, layer, 3), _mod_spec(bsz, layer, 4), _mod_spec(bsz, layer, 5)]
        + [full(a) for a in args],
        out_specs=[xspec,
                   pl.BlockSpec((bsz, d), lambda i: (0, 0)),
                   pl.BlockSpec((CONV_W - 1, bsz, d), lambda i: (0, 0, 0))],
        out_shape=[jax.ShapeDtypeStruct(x.shape, F32),
                   jax.ShapeDtypeStruct((bsz, d), F32),
                   jax.ShapeDtypeStruct((CONV_W - 1, bsz, d), F32)],
        scratch_shapes=[pltpu.VMEM((tt * bsz, d), F32), pltpu.VMEM((tt * bsz, d), F32)],
        compiler_params=_params("arbitrary"),
        name="rglru",
    )(x, mods, mods, mods, *args)


def _rwkv_in_kernel(*refs, has_vres):
    (x_ref, sh_ref, sc_ref, shift0_ref, mu_ref, wrkv_ref, w0_ref, w1_ref, w2_ref, a0_ref, a1_ref,
     a2_ref, g1_ref, g2_ref, kk_ref, ka_ref) = refs[:16]
    refs = refs[16:]
    if has_vres:
        v0_ref, v1_ref, v2_ref, vfirst_ref = refs[:4]
        refs = refs[4:]
    r_out, w_out, k_out, v_out, na_out, b_out, g_out, shift_ref = refs

    @pl.when(pl.program_id(0) == 0)
    def _():
        shift_ref[...] = shift0_ref[...]

    x = x_ref[...]
    tt, bsz, d = x.shape
    tm = tt * bsz
    xf = _rms_mod(x, sh_ref[...], sc_ref[...])
    prev = jnp.concatenate([shift_ref[...][None], xf[:tt - 1]], axis=0)
    shift_ref[...] = xf[tt - 1]
    dx = prev - xf

    def mix(j):
        return (xf + dx * mu_ref[j:j + 1, :][None]).reshape(tm, d).astype(BF16)

    r = _dot(mix(0), wrkv_ref[0])
    k = _dot(mix(1), wrkv_ref[1])
    xm_v = mix(2)
    v = _dot(xm_v, wrkv_ref[2])
    z = w0_ref[...] + _dot(jnp.tanh(_dot(mix(3), w1_ref[...])).astype(BF16), w2_ref[...])
    w_log = -_softplus(-z) - 0.5
    decay = jnp.exp(-jnp.exp(w_log))
    if has_vres:
        mix_v = jax.nn.sigmoid(v0_ref[...] + _dot(_dot(xm_v, v1_ref[...]).astype(BF16), v2_ref[...]))
        v = v + (vfirst_ref[...].reshape(tm, d) - v) * mix_v
    a = jax.nn.sigmoid(a0_ref[...] + _dot(_dot(mix(4), a1_ref[...]).astype(BF16), a2_ref[...]))
    g = _dot(jax.nn.sigmoid(_dot(mix(5), g1_ref[...])).astype(BF16), g2_ref[...])

    kk = k * kk_ref[...]
    norm = jnp.sqrt(_head_sum(kk * kk, _head_ones()))
    kk = kk / jnp.maximum(norm, 1e-12)
    k = k * (1.0 + (a - 1.0) * ka_ref[...])

    shp = (tt, bsz, d)
    r_out[...] = r.reshape(shp)
    w_out[...] = decay.reshape(shp)
    k_out[...] = k.reshape(shp)
    v_out[...] = v.reshape(shp)
    na_out[...] = (-kk).reshape(shp)
    b_out[...] = (kk * a).reshape(shp)
    g_out[...] = g.reshape(shp)


def _rwkv_in(x, mods, layer, shift0, weights, vres):
    t_len, bsz, d = x.shape
    tt = _time_tile(t_len, bsz, RWKV_IN_ROW_TILE)
    xspec = pl.BlockSpec((tt, bsz, d), lambda i: (i, 0, 0))

    def full(a):
        return _resident(a.shape, lambda i, n=a.ndim: (0,) * n)

    args = [shift0] + list(weights)
    in_specs = [xspec, _mod_spec(bsz, layer, 3), _mod_spec(bsz, layer, 4)] + [full(a) for a in args]
    if vres is not None:
        v0, v1, v2, v_first = vres
        args += [v0, v1, v2, v_first]
        in_specs += [full(v0), full(v1), full(v2), xspec]
    act = jax.ShapeDtypeStruct(x.shape, F32)
    return pl.pallas_call(
        functools.partial(_rwkv_in_kernel, has_vres=vres is not None),
        grid=(t_len // tt,),
        in_specs=in_specs,
        out_specs=[xspec] * 7 + [pl.BlockSpec((bsz, d), lambda i: (0, 0))],
        out_shape=[act] * 7 + [jax.ShapeDtypeStruct((bsz, d), F32)],
        compiler_params=_params("arbitrary"),
        name="rwkv_in",
    )(x, mods, mods, *args)


def _wkv_kernel(r_ref, w_ref, k_ref, v_ref, a_ref, b_ref, s0_ref, y_ref, s_ref, sel_scr, ones_scr):
    @pl.when(pl.program_id(1) == 0)
    def _():
        s_ref[...] = s0_ref[...]

    row = lax.broadcasted_iota(jnp.int32, (HEAD_SIZE, LANES), 0)
    lane = lax.broadcasted_iota(jnp.int32, (HEAD_SIZE, LANES), 1)
    sel_scr[...] = (lane % HEAD_SIZE == row).astype(F32)
    ones_scr[...] = _head_ones()
    tb, bb = r_ref.shape[0], r_ref.shape[1]

    def t_body(t, carry):
        def g_body(g, carry):
            b0 = pl.multiple_of(g // HEAD_PAIRS * SUBLANES, SUBLANES)
            p = g % HEAD_PAIRS
            idx = (t, pl.ds(b0, SUBLANES), pl.ds(pl.multiple_of(p * LANES, LANES), LANES))
            rt, wt, kt, vt, at, bt = (ref[idx] for ref in (r_ref, w_ref, k_ref, v_ref, a_ref, b_ref))
            sel = sel_scr[...]
            ones = ones_scr[...]
            ys = []
            for i in range(SUBLANES):
                row = slice(i, i + 1)
                s_old = s_ref[b0 + i, p]
                lhs = jnp.concatenate([s_old * at[row], sel * vt[row]], axis=0).astype(BF16)
                res = _dot(lhs, ones)
                s_new = s_old * wt[row] + res[:HEAD_SIZE] * bt[row] + res[HEAD_SIZE:] * kt[row]
                s_ref[b0 + i, p] = s_new
                y_rep = _dot((s_new * rt[row]).astype(BF16), ones)
                ys.append(jnp.sum(y_rep * sel, axis=0, keepdims=True))
            y_ref[idx] = jnp.concatenate(ys, axis=0)
            return carry

        return lax.fori_loop(0, bb // SUBLANES * HEAD_PAIRS, g_body, carry)

    lax.fori_loop(0, tb, t_body, 0)


def _wkv(r, w, k, v, a, b, s0):
    t_len, bsz, d = r.shape
    bb = min(bsz, 16)
    tb = min(t_len, 32)
    xspec = pl.BlockSpec((tb, bb, d), lambda i, j: (j, i, 0))
    sspec = pl.BlockSpec((bb, HEAD_PAIRS, HEAD_SIZE, LANES), lambda i, j: (i, 0, 0, 0))
    return pl.pallas_call(
        _wkv_kernel,
        grid=(bsz // bb, t_len // tb),
        in_specs=[xspec] * 6 + [sspec],
        out_specs=[xspec, sspec],
        out_shape=[jax.ShapeDtypeStruct(r.shape, F32), jax.ShapeDtypeStruct(s0.shape, F32)],
        scratch_shapes=[pltpu.VMEM((HEAD_SIZE, LANES), F32), pltpu.VMEM((LANES, LANES), BF16)],
        compiler_params=_params("parallel", "arbitrary"),
        name="wkv",
    )(r, w, k, v, a, b, s0)


def _rwkv_out_kernel(y_ref, r_ref, k_ref, v_ref, g_ref, x_ref, gate_ref, lnw_ref, lnb_ref, rk_ref,
                     wo_ref, o_ref):
    x = x_ref[...]
    tt, bsz, d = x.shape
    tm = tt * bsz
    ones = _head_ones()
    y = y_ref[...].reshape(tm, d)
    yc = y - _head_sum(y, ones) * (1.0 / HEAD_SIZE)
    var = _head_sum(yc * yc, ones) * (1.0 / HEAD_SIZE)
    yn = yc * lax.rsqrt(var + GN_EPS) * lnw_ref[...] + lnb_ref[...]
    rk = r_ref[...].reshape(tm, d) * k_ref[...].reshape(tm, d) * rk_ref[...]
    bonus = _head_sum(rk, ones) * v_ref[...].reshape(tm, d)
    out = _dot(((yn + bonus) * g_ref[...].reshape(tm, d)).astype(BF16), wo_ref[...])
    o_ref[...] = x + (1.0 + gate_ref[...])[None] * out.reshape(tt, bsz, d)


def _rwkv_out(y, r, k, v, g, x, mods, layer, ln_w, ln_b, r_k, w_o):
    t_len, bsz, d = x.shape
    tt = _time_tile(t_len, bsz)
    xspec = pl.BlockSpec((tt, bsz, d), lambda i: (i, 0, 0))

    def full(a):
        return _resident(a.shape, lambda i, n=a.ndim: (0,) * n)

    return pl.pallas_call(
        _rwkv_out_kernel,
        grid=(t_len // tt,),
        in_specs=[xspec] * 6 + [_mod_spec(bsz, layer, 5)] + [full(a) for a in (ln_w, ln_b, r_k, w_o)],
        out_specs=xspec,
        out_shape=jax.ShapeDtypeStruct(x.shape, F32),
        compiler_params=_params("parallel"),
        name="rwkv_out",
    )(y, r, k, v, g, x, mods, ln_w, ln_b, r_k, w_o)


def _final_kernel(x_ref, gain_ref, o_ref):
    x = x_ref[...]
    ms = jnp.mean(x * x, axis=-1, keepdims=True)
    o_ref[...] = (x * lax.rsqrt(ms + NORM_EPS)) * gain_ref[...][None]


def _final(x, gain):
    t_len, bsz, d = x.shape
    tt = _time_tile(t_len, bsz)
    xspec = pl.BlockSpec((tt, bsz, d), lambda i: (i, 0, 0))
    return pl.pallas_call(
        _final_kernel,
        grid=(t_len // tt,),
        in_specs=[xspec, pl.BlockSpec((1, d), lambda i: (0, 0))],
        out_specs=xspec,
        out_shape=jax.ShapeDtypeStruct(x.shape, F32),
        compiler_params=_params("parallel"),
        name="final_norm",
    )(x, gain)


def _pack_state(s):
    b = s.shape[0]
    s = s.reshape(b, HEAD_PAIRS, 2, HEAD_SIZE, HEAD_SIZE).transpose(0, 1, 3, 2, 4)
    return s.reshape(b, HEAD_PAIRS, HEAD_SIZE, LANES)


def _unpack_state(s):
    b = s.shape[0]
    s = s.reshape(b, HEAD_PAIRS, HEAD_SIZE, 2, HEAD_SIZE).transpose(0, 1, 3, 2, 4)
    return s.reshape(b, RWKV_HEADS, HEAD_SIZE, HEAD_SIZE)


def _pad_cols(w):
    n = w.shape[-1]
    return jnp.pad(w, [(0, 0)] * (w.ndim - 1) + [(0, -n % LORA_PAD)])


def _pad_rows(w):
    n = w.shape[-2]
    return jnp.pad(w, [(0, 0)] * (w.ndim - 2) + [(0, -n % LORA_PAD), (0, 0)])


def _trunk(x, mods, lru_h0, lru_conv0, rwkv_shift0, rwkv_wkv0, W):
    new_h, new_conv, new_shift, new_wkv = [], [], [], []
    v_first = None
    for layer in range(DEPTH):
        j = layer // 2
        x = _ffn(x, mods, layer, 0, W['ffn_w_in'], W['ffn_w_out'])
        if layer % 2 == 0:
            x, h_last, buf = _lru(x, mods, layer, lru_h0[j], lru_conv0[j], W['lru_w_in'][j],
                                  W['lru_conv_w'][j], W['lru_conv_b'][j], W['lru_gate_w'][j],
                                  W['lru_gate_b'][j], W['lru_lambda'][j], W['lru_w_out'][j])
            new_h.append(h_last)
            new_conv.append(buf)
        else:
            weights = [W[n][j] for n in ('rwkv_mu', 'rwkv_w_rkv', 'rwkv_w0', 'rwkv_w1', 'rwkv_w2',
                                         'rwkv_a0', 'rwkv_a1', 'rwkv_a2', 'rwkv_g1', 'rwkv_g2',
                                         'rwkv_k_k', 'rwkv_k_a')]
            vres = None if j == 0 else (W['rwkv_v0'][j - 1], W['rwkv_v1'][j - 1], W['rwkv_v2'][j - 1],
                                        v_first)
            r, w, k, v, na, b, g, shift_last = _rwkv_in(x, mods, layer, rwkv_shift0[j], weights, vres)
            if v_first is None:
                v_first = v
            y, s_last = _wkv(r, w, k, v, na, b, rwkv_wkv0[j])
            x = _rwkv_out(y, r, k, v, g, x, mods, layer, W['rwkv_ln_w'][j], W['rwkv_ln_b'][j],
                          W['rwkv_r_k'][j], W['rwkv_w_o'][j])
            new_shift.append(shift_last)
            new_wkv.append(s_last)
        x = _ffn(x, mods, layer, 1, W['ffn_w_in'], W['ffn_w_out'])
    y = _final(x, W['final_gain'])
    return y, jnp.stack(new_h), jnp.stack(new_conv), jnp.stack(new_shift), jnp.stack(new_wkv)


def _prep_weights(P):
    row = lambda a: a.reshape(a.shape[0], 1, -1)
    W = {
        'ffn_w_in': P['ffn_w_in'].astype(BF16),
        'ffn_w_out': P['ffn_w_out'].astype(BF16),
        'lru_w_in': P['lru_w_in'].astype(BF16),
        'lru_conv_w': P['lru_conv_w'],
        'lru_conv_b': row(P['lru_conv_b']),
        'lru_gate_w': P['lru_gate_w'].astype(BF16),
        'lru_gate_b': P['lru_gate_b'],
        'lru_lambda': row(P['lru_lambda']),
        'lru_w_out': P['lru_w_out'].astype(BF16),
        'rwkv_mu': P['rwkv_mu'],
        'rwkv_w_rkv': P['rwkv_w_rkv'].astype(BF16),
        'rwkv_w_o': P['rwkv_w_o'].astype(BF16),
        'rwkv_r_k': row(P['rwkv_r_k']),
        'final_gain': P['final_gain'].reshape(1, -1),
    }
    for n in ('rwkv_w0', 'rwkv_a0', 'rwkv_v0', 'rwkv_k_k', 'rwkv_k_a', 'rwkv_ln_w', 'rwkv_ln_b'):
        W[n] = row(P[n])
    for n in ('rwkv_w1', 'rwkv_a1', 'rwkv_v1', 'rwkv_g1'):
        W[n] = _pad_cols(P[n]).astype(BF16)
    for n in ('rwkv_w2', 'rwkv_a2', 'rwkv_v2', 'rwkv_g2'):
        W[n] = _pad_rows(P[n]).astype(BF16)
    return W


def _run_stream(x, mods, lru_h, lru_conv, rwkv_shift, rwkv_wkv, W):
    nw, b = rwkv_wkv.shape[0], rwkv_wkv.shape[1]
    wkv0 = _pack_state(rwkv_wkv.reshape((nw * b,) + rwkv_wkv.shape[2:])).reshape(
        nw, b, HEAD_PAIRS, HEAD_SIZE, LANES)
    y, h, conv, shift, wkv = _trunk(jnp.transpose(x, (1, 0, 2)), mods, lru_h,
                                    jnp.transpose(lru_conv, (0, 2, 1, 3)), rwkv_shift, wkv0, W)
    wkv = _unpack_state(wkv.reshape(nw * b, HEAD_PAIRS, HEAD_SIZE, LANES)).reshape(rwkv_wkv.shape)
    return jnp.transpose(y, (1, 0, 2)), h, jnp.transpose(conv, (0, 2, 1, 3)), shift, wkv


def kernel(x_prompt, x_sample, state_lru_h, state_lru_conv, state_rwkv_shift, state_rwkv_wkv, c_prompt, c_sample, ada_w, ada_b, ffn_w_in, ffn_w_out, lru_w_in, lru_conv_w, lru_conv_b, lru_gate_w, lru_gate_b, lru_lambda, lru_w_out, rwkv_mu, rwkv_w_rkv, rwkv_w_o, rwkv_w0, rwkv_w1, rwkv_w2, rwkv_a0, rwkv_a1, rwkv_a2, rwkv_v0, rwkv_v1, rwkv_v2, rwkv_g1, rwkv_g2, rwkv_k_k, rwkv_k_a, rwkv_r_k, rwkv_ln_w, rwkv_ln_b, final_gain):
    P = dict(ffn_w_in=ffn_w_in, ffn_w_out=ffn_w_out,
             lru_w_in=lru_w_in, lru_conv_w=lru_conv_w, lru_conv_b=lru_conv_b, lru_gate_w=lru_gate_w,
             lru_gate_b=lru_gate_b, lru_lambda=lru_lambda, lru_w_out=lru_w_out,
             rwkv_mu=rwkv_mu, rwkv_w_rkv=rwkv_w_rkv, rwkv_w_o=rwkv_w_o, rwkv_w0=rwkv_w0, rwkv_w1=rwkv_w1,
             rwkv_w2=rwkv_w2, rwkv_a0=rwkv_a0, rwkv_a1=rwkv_a1, rwkv_a2=rwkv_a2, rwkv_v0=rwkv_v0,
             rwkv_v1=rwkv_v1, rwkv_v2=rwkv_v2, rwkv_g1=rwkv_g1, rwkv_g2=rwkv_g2, rwkv_k_k=rwkv_k_k,
             rwkv_k_a=rwkv_k_a, rwkv_r_k=rwkv_r_k, rwkv_ln_w=rwkv_ln_w, rwkv_ln_b=rwkv_ln_b,
             final_gain=final_gain)
    W = _prep_weights(P)
    bp, bs = x_prompt.shape[0], x_sample.shape[0]
    mods = _ada(jnp.concatenate([c_prompt, c_sample], axis=0), ada_w, ada_b)
    n_lru, n_rwkv = state_lru_h.shape[0], state_rwkv_shift.shape[0]
    zeros = lambda *s: jnp.zeros(s, F32)
    out_p = _run_stream(x_prompt, mods[:, :bp], zeros(n_lru, bp, D_RNN), zeros(n_lru, bp, CONV_W - 1, D_RNN),
                        zeros(n_rwkv, bp, D_MODEL), zeros(n_rwkv, bp, RWKV_HEADS, HEAD_SIZE, HEAD_SIZE), W)
    out_s = _run_stream(x_sample, mods[:, bp:], state_lru_h, state_lru_conv, state_rwkv_shift,
                        state_rwkv_wkv, W)
    return (out_p[0], out_s[0]) + out_p[1:] + out_s[1:]
```

```python
import functools

import jax
import jax.numpy as jnp
from jax import lax
from jax.experimental import pallas as pl
from jax.experimental.pallas import tpu as pltpu

D_MODEL = 1024
DEPTH = 4
D_RNN = D_MODEL
LRU_BLOCKS = 4
LRU_BLOCK_W = D_RNN // LRU_BLOCKS
CONV_W = 4
RG_C = 8.0
HEAD_SIZE = 64
RWKV_HEADS = D_MODEL // HEAD_SIZE
D_FF = 2816
FFN_RES = 0.5
N_MOD = 9
NORM_EPS = 1e-6
GN_EPS = HEAD_SIZE * 1e-5

LANES = 128
SUBLANES = 8
HEAD_PAIRS = D_MODEL // LANES
LORA_PAD = 128
VMEM_LIMIT_BYTES = 56 * 1024 * 1024
ROW_TILE = 512
RWKV_IN_ROW_TILE = 256
FFN_CHUNK = 256
ADA_TILE = 1152
SCAN_CARRY_VREGS = 16
WKV_COLS = 6 * SUBLANES
WKV_BATCH_TILE = 16
WKV_TIME_TILE = 32
WKV_TIME_UNROLL = 4
WKV_LAG = 4

F32 = jnp.float32
BF16 = jnp.bfloat16


def _dot(a, b):
    return jnp.dot(a, b, preferred_element_type=F32)


def _params(*sem):
    return pltpu.CompilerParams(dimension_semantics=sem, vmem_limit_bytes=VMEM_LIMIT_BYTES)


def _resident(block_shape, index_map):
    return pl.BlockSpec(block_shape, index_map, pipeline_mode=pl.Buffered(1))


def _rms_mod(x, shift, scale):
    ms = jnp.mean(x * x, axis=-1, keepdims=True)
    return (x * lax.rsqrt(ms + NORM_EPS)) * (1.0 + scale)[None] + shift[None]


def _softplus(x):
    return jnp.maximum(x, 0.0) + jnp.log1p(jnp.exp(-jnp.abs(x)))


def _head_ones():
    r = lax.broadcasted_iota(jnp.int32, (LANES, LANES), 0) // HEAD_SIZE
    c = lax.broadcasted_iota(jnp.int32, (LANES, LANES), 1) // HEAD_SIZE
    return (r == c).astype(BF16)


def _head_sum(x, ones):
    outs = []
    for c in range(HEAD_PAIRS):
        xc = x[:, c * LANES:(c + 1) * LANES]
        hi = xc.astype(BF16)
        lo = (xc - hi.astype(F32)).astype(BF16)
        outs.append(_dot(hi, ones) + _dot(lo, ones))
    return jnp.concatenate(outs, axis=-1)


def _ada_kernel(c_ref, w_ref, b_ref, o_ref):
    c = c_ref[...]
    s = (c * jax.nn.sigmoid(c)).astype(BF16)
    o_ref[...] = _dot(s, w_ref[...].astype(BF16)) + b_ref[...]


def _ada(c_all, ada_w, ada_b):
    n = c_all.shape[0]
    nd = N_MOD * D_MODEL
    return pl.pallas_call(
        _ada_kernel,
        grid=(DEPTH, nd // ADA_TILE),
        in_specs=[
            pl.BlockSpec((n, D_MODEL), lambda l, j: (0, 0)),
            pl.BlockSpec((None, D_MODEL, ADA_TILE), lambda l, j: (l, 0, j)),
            pl.BlockSpec((None, 1, ADA_TILE), lambda l, j: (l, 0, j)),
        ],
        out_specs=pl.BlockSpec((None, n, ADA_TILE), lambda l, j: (l, 0, j)),
        out_shape=jax.ShapeDtypeStruct((DEPTH, n, nd), F32),
        compiler_params=_params("parallel", "parallel"),
        name="ada_mod",
    )(c_all, ada_w, ada_b.reshape(DEPTH, 1, nd))


def _mod_spec(bsz, layer, m):
    return pl.BlockSpec((None, bsz, D_MODEL), lambda *_: (layer, 0, m))


def _time_tile(t_len, bsz, rows=ROW_TILE):
    return max(1, min(t_len, rows // bsz))


def _ffn_kernel(x_ref, sh_ref, sc_ref, g_ref, win_ref, wout_ref, o_ref):
    x = x_ref[...]
    tt, bsz, d = x.shape
    h = _rms_mod(x, sh_ref[...], sc_ref[...]).reshape(tt * bsz, d).astype(BF16)
    acc = jnp.zeros((tt * bsz, d), F32)
    for j in range(D_FF // FFN_CHUNK):
        lo = j * FFN_CHUNK
        gate = _dot(h, win_ref[:, lo:lo + FFN_CHUNK])
        up = _dot(h, win_ref[:, D_FF + lo:D_FF + lo + FFN_CHUNK])
        act = (gate * jax.nn.sigmoid(gate) * up).astype(BF16)
        acc = acc + _dot(act, wout_ref[lo:lo + FFN_CHUNK, :])
    o_ref[...] = x + (FFN_RES * (1.0 + g_ref[...]))[None] * acc.reshape(tt, bsz, d)


def _ffn(x, mods, layer, sub, w_in, w_out):
    t_len, bsz, d = x.shape
    tt = _time_tile(t_len, bsz)
    m0 = 0 if sub == 0 else 6
    xspec = pl.BlockSpec((tt, bsz, d), lambda i: (i, 0, 0))
    return pl.pallas_call(
        _ffn_kernel,
        grid=(t_len // tt,),
        in_specs=[
            xspec,
            _mod_spec(bsz, layer, m0), _mod_spec(bsz, layer, m0 + 1), _mod_spec(bsz, layer, m0 + 2),
            _resident((None, None, d, 2 * D_FF), lambda i: (layer, sub, 0, 0)),
            _resident((None, None, D_FF, d), lambda i: (layer, sub, 0, 0)),
        ],
        out_specs=xspec,
        out_shape=jax.ShapeDtypeStruct(x.shape, F32),
        compiler_params=_params("parallel"),
        name="ffn",
    )(x, mods, mods, mods, w_in, w_out)


def _lru_kernel(x_ref, sh_ref, sc_ref, g_ref, h0_ref, conv0_ref, win_ref, cw_ref, cb_ref, gw_ref,
                gb_ref, lam_ref, wout_ref, o_ref, hlast_ref, buf_ref, a_scr, u_scr):
    @pl.when(pl.program_id(0) == 0)
    def _():
        hlast_ref[...] = h0_ref[...]
        buf_ref[...] = conv0_ref[...]

    x = x_ref[...]
    tt, bsz, d = x.shape
    tm = tt * bsz
    h = _rms_mod(x, sh_ref[...], sc_ref[...]).reshape(tm, d).astype(BF16)
    proj = _dot(h, win_ref[...])
    gate_branch = proj[:, :d]
    xp = jnp.concatenate([buf_ref[...], proj[:, d:].reshape(tt, bsz, d)], axis=0)
    cw = cw_ref[...]
    xc = cb_ref[...][None]
    for j in range(CONV_W):
        xc = xc + xp[j:j + tt] * cw[j:j + 1][None]
    buf_ref[...] = xp[tt:tt + CONV_W - 1]
    xc = xc.reshape(tm, d)

    lam = lam_ref[...]
    log_sig_lam = -_softplus(-lam)
    for n in range(LRU_BLOCKS):
        sl = slice(n * LRU_BLOCK_W, (n + 1) * LRU_BLOCK_W)
        xb = xc[:, sl]
        gates = _dot(xb.astype(BF16), gw_ref[n]) + gb_ref[n:n + 1, :]
        r = jax.nn.sigmoid(gates[:, :LRU_BLOCK_W])
        ig = jax.nn.sigmoid(gates[:, LRU_BLOCK_W:])
        log_a = RG_C * r * log_sig_lam[:, sl]
        a_scr[:, sl] = jnp.exp(log_a)
        th = jnp.tanh(log_a)
        u_scr[:, sl] = jnp.sqrt(-2.0 * th / (1.0 - th)) * (ig * xb)

    cwid = min(d, SCAN_CARRY_VREGS * SUBLANES * LANES // bsz)
    for c in range(d // cwid):
        ls = slice(c * cwid, (c + 1) * cwid)

        def step(t, hc, ls=ls):
            rows = pl.ds(pl.multiple_of(t * bsz, bsz), bsz)
            hc = a_scr[rows, ls] * hc + u_scr[rows, ls]
            u_scr[rows, ls] = hc
            return hc

        hlast_ref[:, ls] = lax.fori_loop(0, tt, step, hlast_ref[:, ls])

    gelu = 0.5 * gate_branch * (1.0 + jnp.tanh(
        0.7978845608028654 * (gate_branch + 0.044715 * (gate_branch * gate_branch * gate_branch))))
    y = _dot((u_scr[...] * gelu).astype(BF16), wout_ref[...])
    o_ref[...] = x + (1.0 + g_ref[...])[None] * y.reshape(tt, bsz, d)


def _lru(x, mods, layer, h0, conv0, w_in, conv_w, conv_b, gate_w, gate_b, lam, w_out):
    t_len, bsz, d = x.shape
    tt = _time_tile(t_len, bsz)
    xspec = pl.BlockSpec((tt, bsz, d), lambda i: (i, 0, 0))

    def full(a):
        return _resident(a.shape, lambda i, n=a.ndim: (0,) * n)

    args = (h0, conv0, w_in, conv_w, conv_b, gate_w, gate_b, lam, w_out)
    return pl.pallas_call(
        _lru_kernel,
        grid=(t_len // tt,),
        in_specs=[xspec, _mod_spec(bsz, layer, 3), _mod_spec(bsz, layer, 4), _mod_spec(bsz, layer, 5)]
        + [full(a) for a in args],
        out_specs=[xspec,
                   pl.BlockSpec((bsz, d), lambda i: (0, 0)),
                   pl.BlockSpec((CONV_W - 1, bsz, d), lambda i: (0, 0, 0))],
        out_shape=[jax.ShapeDtypeStruct(x.shape, F32),
                   jax.ShapeDtypeStruct((bsz, d), F32),
                   jax.ShapeDtypeStruct((CONV_W - 1, bsz, d), F32)],
        scratch_shapes=[pltpu.VMEM((tt * bsz, d), F32), pltpu.VMEM((tt * bsz, d), F32)],
        compiler_params=_params("arbitrary"),
        name="rglru",
    )(x, mods, mods, mods, *args)


def _rwkv_in_kernel(*refs, has_vres):
    (x_ref, sh_ref, sc_ref, shift0_ref, mu_ref, wrkv_ref, w0_ref, w1_ref, w2_ref, a0_ref, a1_ref,
     a2_ref, g1_ref, g2_ref, kk_ref, ka_ref) = refs[:16]
    refs = refs[16:]
    if has_vres:
        v0_ref, v1_ref, v2_ref, vfirst_ref = refs[:4]
        refs = refs[4:]
    r_out, w_out, k_out, v_out, na_out, b_out, g_out, shift_ref = refs

    @pl.when(pl.program_id(0) == 0)
    def _():
        shift_ref[...] = shift0_ref[...]

    x = x_ref[...]
    tt, bsz, d = x.shape
    tm = tt * bsz
    xf = _rms_mod(x, sh_ref[...], sc_ref[...])
    prev = jnp.concatenate([shift_ref[...][None], xf[:tt - 1]], axis=0)
    shift_ref[...] = xf[tt - 1]
    dx = prev - xf

    def mix(j):
        return (xf + dx * mu_ref[j:j + 1, :][None]).reshape(tm, d).astype(BF16)

    r = _dot(mix(0), wrkv_ref[0])
    k = _dot(mix(1), wrkv_ref[1])
    xm_v = mix(2)
    v = _dot(xm_v, wrkv_ref[2])
    z = w0_ref[...] + _dot(jnp.tanh(_dot(mix(3), w1_ref[...])).astype(BF16), w2_ref[...])
    w_log = -_softplus(-z) - 0.5
    decay = jnp.exp(-jnp.exp(w_log))
    if has_vres:
        mix_v = jax.nn.sigmoid(v0_ref[...] + _dot(_dot(xm_v, v1_ref[...]).astype(BF16), v2_ref[...]))
        v = v + (vfirst_ref[...].reshape(tm, d) - v) * mix_v
    a = jax.nn.sigmoid(a0_ref[...] + _dot(_dot(mix(4), a1_ref[...]).astype(BF16), a2_ref[...]))
    g = _dot(jax.nn.sigmoid(_dot(mix(5), g1_ref[...])).astype(BF16), g2_ref[...])

    kk = k * kk_ref[...]
    norm = jnp.sqrt(_head_sum(kk * kk, _head_ones()))
    kk = kk / jnp.maximum(norm, 1e-12)
    k = k * (1.0 + (a - 1.0) * ka_ref[...])

    shp = (tt, bsz, d)
    r_out[...] = r.reshape(shp)
    w_out[...] = decay.reshape(shp)
    k_out[...] = k.reshape(shp)
    v_out[...] = v.reshape(shp)
    na_out[...] = (-kk).reshape(shp)
    b_out[...] = (kk * a).reshape(shp)
    g_out[...] = g.reshape(shp)


def _rwkv_in(x, mods, layer, shift0, weights, vres):
    t_len, bsz, d = x.shape
    tt = _time_tile(t_len, bsz, RWKV_IN_ROW_TILE)
    xspec = pl.BlockSpec((tt, bsz, d), lambda i: (i, 0, 0))

    def full(a):
        return _resident(a.shape, lambda i, n=a.ndim: (0,) * n)

    args = [shift0] + list(weights)
    in_specs = [xspec, _mod_spec(bsz, layer, 3), _mod_spec(bsz, layer, 4)] + [full(a) for a in args]
    if vres is not None:
        v0, v1, v2, v_first = vres
        args += [v0, v1, v2, v_first]
        in_specs += [full(v0), full(v1), full(v2), xspec]
    act = jax.ShapeDtypeStruct(x.shape, F32)
    return pl.pallas_call(
        functools.partial(_rwkv_in_kernel, has_vres=vres is not None),
        grid=(t_len // tt,),
        in_specs=in_specs,
        out_specs=[xspec] * 7 + [pl.BlockSpec((bsz, d), lambda i: (0, 0))],
        out_shape=[act] * 7 + [jax.ShapeDtypeStruct((bsz, d), F32)],
        compiler_params=_params("arbitrary"),
        name="rwkv_in",
    )(x, mods, mods, *args)


def _wkv_kernel(r_ref, w_ref, k_ref, v_ref, a_ref, b_ref, s0_ref, y_ref, s_ref):
    @pl.when(pl.program_id(1) == 0)
    def _():
        s_ref[...] = s0_ref[...]

    tb, bb = r_ref.shape[0], r_ref.shape[1]
    head0 = lax.broadcasted_iota(jnp.int32, (SUBLANES, LANES), 1) < HEAD_SIZE
    rows = lax.broadcasted_iota(jnp.int32, (WKV_COLS, LANES), 0)
    state_rows = rows < 4 * SUBLANES
    ones = _head_ones()
    eye = (lax.broadcasted_iota(jnp.int32, (HEAD_SIZE, LANES), 0)
           == lax.broadcasted_iota(jnp.int32, (HEAD_SIZE, LANES), 1)).astype(BF16)
    zero8 = jnp.zeros((SUBLANES, LANES), F32)
    contract_lanes = (((1,), (1,)), ((), ()))

    def split(x):
        return [jnp.where(head0, x, 0.0), jnp.where(head0, 0.0, x)]

    def stage_a(t, b0, p):
        idx = (t, pl.ds(b0, SUBLANES), pl.ds(p * LANES, LANES))
        rt, wt, vt, at = (ref[idx] for ref in (r_ref, w_ref, v_ref, a_ref))
        v_swapped = pltpu.roll(vt, HEAD_SIZE, 1)
        lmat = jnp.concatenate(split(at) + split(wt * rt)
                               + [jnp.where(head0, vt, 0.0), jnp.where(head0, v_swapped, 0.0)],
                               axis=0).astype(BF16)
        none = jnp.zeros_like(lmat)
        proj = lax.dot_general(eye, jnp.where(state_rows, none, lmat), contract_lanes,
                               preferred_element_type=F32)
        for i in range(SUBLANES):
            l_i = jnp.where(state_rows & (rows % SUBLANES == i), lmat, none)
            proj = proj + lax.dot_general(s_ref[b0 + i, p].astype(BF16), l_i, contract_lanes,
                                          preferred_element_type=F32)
        return proj

    def stage_b(t, b0, p, proj):
        idx = (t, pl.ds(b0, SUBLANES), pl.ds(p * LANES, LANES))
        rt, wt, kt, vt, bt = (ref[idx] for ref in (r_ref, w_ref, k_ref, v_ref, b_ref))
        ymat = jnp.concatenate(split(bt) + [zero8, zero8] + split(kt), axis=0).astype(BF16)
        none = jnp.zeros_like(ymat)
        pb = proj.astype(BF16)
        for i in range(SUBLANES):
            y_i = jnp.where(rows % SUBLANES == i, ymat, none)
            s_ref[b0 + i, p] = s_ref[b0 + i, p] * wt[i:i + 1] + _dot(pb, y_i)
        pt = proj.T
        sa = jnp.concatenate([pt[0:SUBLANES], pt[SUBLANES:2 * SUBLANES]], axis=1)
        swr = jnp.concatenate([pt[2 * SUBLANES:3 * SUBLANES], pt[3 * SUBLANES:4 * SUBLANES]], axis=1)
        br_kr = _dot(jnp.concatenate([bt * rt, kt * rt], axis=0).astype(BF16), ones)
        y_ref[idx] = swr + sa * br_kr[:SUBLANES] + vt * br_kr[SUBLANES:]

    tiles = [(g * SUBLANES, p) for g in range(bb // SUBLANES) for p in range(HEAD_PAIRS)]
    unroll = WKV_TIME_UNROLL if tb % WKV_TIME_UNROLL == 0 else 1

    def t_body(tu, carry):
        work = [(tu * unroll + u, b0, p) for u in range(unroll) for (b0, p) in tiles]
        pending = {}
        for step in range(len(work) + WKV_LAG):
            if step < len(work):
                pending[step] = stage_a(*work[step])
            if step >= WKV_LAG:
                stage_b(*work[step - WKV_LAG], pending.pop(step - WKV_LAG))
        return carry

    lax.fori_loop(0, tb // unroll, t_body, 0)


def _wkv(r, w, k, v, a, b, s0):
    t_len, bsz, d = r.shape
    bb = min(bsz, WKV_BATCH_TILE)
    tb = min(t_len, WKV_TIME_TILE)
    xspec = pl.BlockSpec((tb, bb, d), lambda i, j: (j, i, 0))
    sspec = pl.BlockSpec((bb, HEAD_PAIRS, HEAD_SIZE, LANES), lambda i, j: (i, 0, 0, 0))
    return pl.pallas_call(
        _wkv_kernel,
        grid=(bsz // bb, t_len // tb),
        in_specs=[xspec] * 6 + [sspec],
        out_specs=[xspec, sspec],
        out_shape=[jax.ShapeDtypeStruct(r.shape, F32), jax.ShapeDtypeStruct(s0.shape, F32)],
        compiler_params=_params("parallel", "arbitrary"),
        name="wkv",
    )(r, w, k, v, a, b, s0)


def _rwkv_out_kernel(y_ref, r_ref, k_ref, v_ref, g_ref, x_ref, gate_ref, lnw_ref, lnb_ref, rk_ref,
                     wo_ref, o_ref):
    x = x_ref[...]
    tt, bsz, d = x.shape
    tm = tt * bsz
    ones = _head_ones()
    y = y_ref[...].reshape(tm, d)
    yc = y - _head_sum(y, ones) * (1.0 / HEAD_SIZE)
    var = _head_sum(yc * yc, ones) * (1.0 / HEAD_SIZE)
    yn = yc * lax.rsqrt(var + GN_EPS) * lnw_ref[...] + lnb_ref[...]
    rk = r_ref[...].reshape(tm, d) * k_ref[...].reshape(tm, d) * rk_ref[...]
    bonus = _head_sum(rk, ones) * v_ref[...].reshape(tm, d)
    out = _dot(((yn + bonus) * g_ref[...].reshape(tm, d)).astype(BF16), wo_ref[...])
    o_ref[...] = x + (1.0 + gate_ref[...])[None] * out.reshape(tt, bsz, d)


def _rwkv_out(y, r, k, v, g, x, mods, layer, ln_w, ln_b, r_k, w_o):
    t_len, bsz, d = x.shape
    tt = _time_tile(t_len, bsz)
    xspec = pl.BlockSpec((tt, bsz, d), lambda i: (i, 0, 0))

    def full(a):
        return _resident(a.shape, lambda i, n=a.ndim: (0,) * n)

    return pl.pallas_call(
        _rwkv_out_kernel,
        grid=(t_len // tt,),
        in_specs=[xspec] * 6 + [_mod_spec(bsz, layer, 5)] + [full(a) for a in (ln_w, ln_b, r_k, w_o)],
        out_specs=xspec,
        out_shape=jax.ShapeDtypeStruct(x.shape, F32),
        compiler_params=_params("parallel"),
        name="rwkv_out",
    )(y, r, k, v, g, x, mods, ln_w, ln_b, r_k, w_o)


def _final_kernel(x_ref, gain_ref, o_ref):
    x = x_ref[...]
    ms = jnp.mean(x * x, axis=-1, keepdims=True)
    o_ref[...] = (x * lax.rsqrt(ms + NORM_EPS)) * gain_ref[...][None]


def _final(x, gain):
    t_len, bsz, d = x.shape
    tt = _time_tile(t_len, bsz)
    xspec = pl.BlockSpec((tt, bsz, d), lambda i: (i, 0, 0))
    return pl.pallas_call(
        _final_kernel,
        grid=(t_len // tt,),
        in_specs=[xspec, pl.BlockSpec((1, d), lambda i: (0, 0))],
        out_specs=xspec,
        out_shape=jax.ShapeDtypeStruct(x.shape, F32),
        compiler_params=_params("parallel"),
        name="final_norm",
    )(x, gain)


def _pack_state(s):
    b = s.shape[0]
    s = s.reshape(b, HEAD_PAIRS, 2, HEAD_SIZE, HEAD_SIZE).transpose(0, 1, 3, 2, 4)
    return s.reshape(b, HEAD_PAIRS, HEAD_SIZE, LANES)


def _unpack_state(s):
    b = s.shape[0]
    s = s.reshape(b, HEAD_PAIRS, HEAD_SIZE, 2, HEAD_SIZE).transpose(0, 1, 3, 2, 4)
    return s.reshape(b, RWKV_HEADS, HEAD_SIZE, HEAD_SIZE)


def _pad_cols(w):
    n = w.shape[-1]
    return jnp.pad(w, [(0, 0)] * (w.ndim - 1) + [(0, -n % LORA_PAD)])


def _pad_rows(w):
    n = w.shape[-2]
    return jnp.pad(w, [(0, 0)] * (w.ndim - 2) + [(0, -n % LORA_PAD), (0, 0)])


def _trunk(x, mods, lru_h0, lru_conv0, rwkv_shift0, rwkv_wkv0, W):
    new_h, new_conv, new_shift, new_wkv = [], [], [], []
    v_first = None
    for layer in range(DEPTH):
        j = layer // 2
        x = _ffn(x, mods, layer, 0, W['ffn_w_in'], W['ffn_w_out'])
        if layer % 2 == 0:
            x, h_last, buf = _lru(x, mods, layer, lru_h0[j], lru_conv0[j], W['lru_w_in'][j],
                                  W['lru_conv_w'][j], W['lru_conv_b'][j], W['lru_gate_w'][j],
                                  W['lru_gate_b'][j], W['lru_lambda'][j], W['lru_w_out'][j])
            new_h.append(h_last)
            new_conv.append(buf)
        else:
            weights = [W[n][j] for n in ('rwkv_mu', 'rwkv_w_rkv', 'rwkv_w0', 'rwkv_w1', 'rwkv_w2',
                                         'rwkv_a0', 'rwkv_a1', 'rwkv_a2', 'rwkv_g1', 'rwkv_g2',
                                         'rwkv_k_k', 'rwkv_k_a')]
            vres = None if j == 0 else (W['rwkv_v0'][j - 1], W['rwkv_v1'][j - 1], W['rwkv_v2'][j - 1],
                                        v_first)
            r, w, k, v, na, b, g, shift_last = _rwkv_in(x, mods, layer, rwkv_shift0[j], weights, vres)
            if v_first is None:
                v_first = v
            y, s_last = _wkv(r, w, k, v, na, b, rwkv_wkv0[j])
            x = _rwkv_out(y, r, k, v, g, x, mods, layer, W['rwkv_ln_w'][j], W['rwkv_ln_b'][j],
                          W['rwkv_r_k'][j], W['rwkv_w_o'][j])
            new_shift.append(shift_last)
            new_wkv.append(s_last)
        x = _ffn(x, mods, layer, 1, W['ffn_w_in'], W['ffn_w_out'])
    y = _final(x, W['final_gain'])
    return y, jnp.stack(new_h), jnp.stack(new_conv), jnp.stack(new_shift), jnp.stack(new_wkv)


def _prep_weights(P):
    row = lambda a: a.reshape(a.shape[0], 1, -1)
    W = {
        'ffn_w_in': P['ffn_w_in'].astype(BF16),
        'ffn_w_out': P['ffn_w_out'].astype(BF16),
        'lru_w_in': P['lru_w_in'].astype(BF16),
        'lru_conv_w': P['lru_conv_w'],
        'lru_conv_b': row(P['lru_conv_b']),
        'lru_gate_w': P['lru_gate_w'].astype(BF16),
        'lru_gate_b': P['lru_gate_b'],
        'lru_lambda': row(P['lru_lambda']),
        'lru_w_out': P['lru_w_out'].astype(BF16),
        'rwkv_mu': P['rwkv_mu'],
        'rwkv_w_rkv': P['rwkv_w_rkv'].astype(BF16),
        'rwkv_w_o': P['rwkv_w_o'].astype(BF16),
        'rwkv_r_k': row(P['rwkv_r_k']),
        'final_gain': P['final_gain'].reshape(1, -1),
    }
    for n in ('rwkv_w0', 'rwkv_a0', 'rwkv_v0', 'rwkv_k_k', 'rwkv_k_a', 'rwkv_ln_w', 'rwkv_ln_b'):
        W[n] = row(P[n])
    for n in ('rwkv_w1', 'rwkv_a1', 'rwkv_v1', 'rwkv_g1'):
        W[n] = _pad_cols(P[n]).astype(BF16)
    for n in ('rwkv_w2', 'rwkv_a2', 'rwkv_v2', 'rwkv_g2'):
        W[n] = _pad_rows(P[n]).astype(BF16)
    return W


def _run_stream(x, mods, lru_h, lru_conv, rwkv_shift, rwkv_wkv, W):
    nw, b = rwkv_wkv.shape[0], rwkv_wkv.shape[1]
    wkv0 = _pack_state(rwkv_wkv.reshape((nw * b,) + rwkv_wkv.shape[2:])).reshape(
        nw, b, HEAD_PAIRS, HEAD_SIZE, LANES)
    y, h, conv, shift, wkv = _trunk(jnp.transpose(x, (1, 0, 2)), mods, lru_h,
                                    jnp.transpose(lru_conv, (0, 2, 1, 3)), rwkv_shift, wkv0, W)
    wkv = _unpack_state(wkv.reshape(nw * b, HEAD_PAIRS, HEAD_SIZE, LANES)).reshape(rwkv_wkv.shape)
    return jnp.transpose(y, (1, 0, 2)), h, jnp.transpose(conv, (0, 2, 1, 3)), shift, wkv


def kernel(x_prompt, x_sample, state_lru_h, state_lru_conv, state_rwkv_shift, state_rwkv_wkv, c_prompt, c_sample, ada_w, ada_b, ffn_w_in, ffn_w_out, lru_w_in, lru_conv_w, lru_conv_b, lru_gate_w, lru_gate_b, lru_lambda, lru_w_out, rwkv_mu, rwkv_w_rkv, rwkv_w_o, rwkv_w0, rwkv_w1, rwkv_w2, rwkv_a0, rwkv_a1, rwkv_a2, rwkv_v0, rwkv_v1, rwkv_v2, rwkv_g1, rwkv_g2, rwkv_k_k, rwkv_k_a, rwkv_r_k, rwkv_ln_w, rwkv_ln_b, final_gain):
    P = dict(ffn_w_in=ffn_w_in, ffn_w_out=ffn_w_out,
             lru_w_in=lru_w_in, lru_conv_w=lru_conv_w, lru_conv_b=lru_conv_b, lru_gate_w=lru_gate_w,
             lru_gate_b=lru_gate_b, lru_lambda=lru_lambda, lru_w_out=lru_w_out,
             rwkv_mu=rwkv_mu, rwkv_w_rkv=rwkv_w_rkv, rwkv_w_o=rwkv_w_o, rwkv_w0=rwkv_w0, rwkv_w1=rwkv_w1,
             rwkv_w2=rwkv_w2, rwkv_a0=rwkv_a0, rwkv_a1=rwkv_a1, rwkv_a2=rwkv_a2, rwkv_v0=rwkv_v0,
             rwkv_v1=rwkv_v1, rwkv_v2=rwkv_v2, rwkv_g1=rwkv_g1, rwkv_g2=rwkv_g2, rwkv_k_k=rwkv_k_k,
             rwkv_k_a=rwkv_k_a, rwkv_r_k=rwkv_r_k, rwkv_ln_w=rwkv_ln_w, rwkv_ln_b=rwkv_ln_b,
             final_gain=final_gain)
    W = _prep_weights(P)
    bp, bs = x_prompt.shape[0], x_sample.shape[0]
    mods = _ada(jnp.concatenate([c_prompt, c_sample], axis=0), ada_w, ada_b)
    n_lru, n_rwkv = state_lru_h.shape[0], state_rwkv_shift.shape[0]
    zeros = lambda *s: jnp.zeros(s, F32)
    out_p = _run_stream(x_prompt, mods[:, :bp], zeros(n_lru, bp, D_RNN), zeros(n_lru, bp, CONV_W - 1, D_RNN),
                        zeros(n_rwkv, bp, D_MODEL), zeros(n_rwkv, bp, RWKV_HEADS, HEAD_SIZE, HEAD_SIZE), W)
    out_s = _run_stream(x_sample, mods[:, bp:], state_lru_h, state_lru_conv, state_rwkv_shift,
                        state_rwkv_wkv, W)
    return (out_p[0], out_s[0]) + out_p[1:] + out_s[1:]
```

```python
import functools

import jax
import jax.numpy as jnp
from jax import lax
from jax.experimental import pallas as pl
from jax.experimental.pallas import tpu as pltpu

D_MODEL = 1024
DEPTH = 4
D_RNN = D_MODEL
LRU_BLOCKS = 4
LRU_BLOCK_W = D_RNN // LRU_BLOCKS
CONV_W = 4
RG_C = 8.0
HEAD_SIZE = 64
RWKV_HEADS = D_MODEL // HEAD_SIZE
D_FF = 2816
FFN_RES = 0.5
N_MOD = 9
NORM_EPS = 1e-6
GN_EPS = HEAD_SIZE * 1e-5

LANES = 128
SUBLANES = 8
HEAD_PAIRS = D_MODEL // LANES
LORA_PAD = 128
VMEM_LIMIT_BYTES = 56 * 1024 * 1024
ROW_TILE = 512
RWKV_IN_ROW_TILE = 256
FFN_CHUNK = 256
ADA_TILE = 1152
SCAN_CARRY_VREGS = 16
WKV_STATE_COLS = 4 * SUBLANES
WKV_COLS = 6 * SUBLANES
WKV_BATCH_TILE = 8
WKV_TIME_TILE = 32
WKV_TIME_UNROLL = 8
WKV_LAG = 6

F32 = jnp.float32
BF16 = jnp.bfloat16


def _dot(a, b):
    return jnp.dot(a, b, preferred_element_type=F32)


def _params(*sem):
    return pltpu.CompilerParams(dimension_semantics=sem, vmem_limit_bytes=VMEM_LIMIT_BYTES)


def _resident(block_shape, index_map):
    return pl.BlockSpec(block_shape, index_map, pipeline_mode=pl.Buffered(1))


def _rms_mod(x, shift, scale):
    ms = jnp.mean(x * x, axis=-1, keepdims=True)
    return (x * lax.rsqrt(ms + NORM_EPS)) * (1.0 + scale)[None] + shift[None]


def _softplus(x):
    return jnp.maximum(x, 0.0) + jnp.log1p(jnp.exp(-jnp.abs(x)))


def _head_ones():
    r = lax.broadcasted_iota(jnp.int32, (LANES, LANES), 0) // HEAD_SIZE
    c = lax.broadcasted_iota(jnp.int32, (LANES, LANES), 1) // HEAD_SIZE
    return (r == c).astype(BF16)


def _head_sum(x, ones):
    outs = []
    for c in range(HEAD_PAIRS):
        xc = x[:, c * LANES:(c + 1) * LANES]
        hi = xc.astype(BF16)
        lo = (xc - hi.astype(F32)).astype(BF16)
        outs.append(_dot(hi, ones) + _dot(lo, ones))
    return jnp.concatenate(outs, axis=-1)


def _ada_kernel(c_ref, w_ref, b_ref, o_ref):
    c = c_ref[...]
    s = (c * jax.nn.sigmoid(c)).astype(BF16)
    o_ref[...] = _dot(s, w_ref[...].astype(BF16)) + b_ref[...]


def _ada(c_all, ada_w, ada_b):
    n = c_all.shape[0]
    nd = N_MOD * D_MODEL
    return pl.pallas_call(
        _ada_kernel,
        grid=(DEPTH, nd // ADA_TILE),
        in_specs=[
            pl.BlockSpec((n, D_MODEL), lambda l, j: (0, 0)),
            pl.BlockSpec((None, D_MODEL, ADA_TILE), lambda l, j: (l, 0, j)),
            pl.BlockSpec((None, 1, ADA_TILE), lambda l, j: (l, 0, j)),
        ],
        out_specs=pl.BlockSpec((None, n, ADA_TILE), lambda l, j: (l, 0, j)),
        out_shape=jax.ShapeDtypeStruct((DEPTH, n, nd), F32),
        compiler_params=_params("parallel", "parallel"),
        name="ada_mod",
    )(c_all, ada_w, ada_b.reshape(DEPTH, 1, nd))


def _mod_spec(bsz, layer, m):
    return pl.BlockSpec((None, bsz, D_MODEL), lambda *_: (layer, 0, m))


def _time_tile(t_len, bsz, rows=ROW_TILE):
    return max(1, min(t_len, rows // bsz))


def _ffn_kernel(x_ref, sh_ref, sc_ref, g_ref, win_ref, wout_ref, o_ref, *, batch_major_in):
    x = x_ref[...]
    if batch_major_in:
        x = jnp.swapaxes(x, 0, 1)
    tt, bsz, d = x.shape
    h = _rms_mod(x, sh_ref[...], sc_ref[...]).reshape(tt * bsz, d).astype(BF16)
    acc = jnp.zeros((tt * bsz, d), F32)
    for j in range(D_FF // FFN_CHUNK):
        lo = j * FFN_CHUNK
        gate = _dot(h, win_ref[:, lo:lo + FFN_CHUNK])
        up = _dot(h, win_ref[:, D_FF + lo:D_FF + lo + FFN_CHUNK])
        act = (gate * jax.nn.sigmoid(gate) * up).astype(BF16)
        acc = acc + _dot(act, wout_ref[lo:lo + FFN_CHUNK, :])
    o_ref[...] = x + (FFN_RES * (1.0 + g_ref[...]))[None] * acc.reshape(tt, bsz, d)


def _ffn(x, mods, layer, sub, w_in, w_out, batch_major_in=False):
    if batch_major_in:
        bsz, t_len, d = x.shape
    else:
        t_len, bsz, d = x.shape
    tt = _time_tile(t_len, bsz)
    m0 = 0 if sub == 0 else 6
    xspec = pl.BlockSpec((tt, bsz, d), lambda i: (i, 0, 0))
    return pl.pallas_call(
        functools.partial(_ffn_kernel, batch_major_in=batch_major_in),
        grid=(t_len // tt,),
        in_specs=[
            pl.BlockSpec((bsz, tt, d), lambda i: (0, i, 0)) if batch_major_in else xspec,
            _mod_spec(bsz, layer, m0), _mod_spec(bsz, layer, m0 + 1), _mod_spec(bsz, layer, m0 + 2),
            _resident((None, None, d, 2 * D_FF), lambda i: (layer, sub, 0, 0)),
            _resident((None, None, D_FF, d), lambda i: (layer, sub, 0, 0)),
        ],
        out_specs=xspec,
        out_shape=jax.ShapeDtypeStruct((t_len, bsz, d), F32),
        compiler_params=_params("parallel"),
        name="ffn",
    )(x, mods, mods, mods, w_in, w_out)


def _lru_kernel(x_ref, sh_ref, sc_ref, g_ref, h0_ref, conv0_ref, win_ref, cw_ref, cb_ref, gw_ref,
                gb_ref, lam_ref, wout_ref, o_ref, hlast_ref, buf_ref, a_scr, u_scr):
    @pl.when(pl.program_id(0) == 0)
    def _():
        hlast_ref[...] = h0_ref[...]
        buf_ref[...] = conv0_ref[...]

    x = x_ref[...]
    tt, bsz, d = x.shape
    tm = tt * bsz
    h = _rms_mod(x, sh_ref[...], sc_ref[...]).reshape(tm, d).astype(BF16)
    proj = _dot(h, win_ref[...])
    gate_branch = proj[:, :d]
    xp = jnp.concatenate([buf_ref[...], proj[:, d:].reshape(tt, bsz, d)], axis=0)
    cw = cw_ref[...]
    xc = cb_ref[...][None]
    for j in range(CONV_W):
        xc = xc + xp[j:j + tt] * cw[j:j + 1][None]
    buf_ref[...] = xp[tt:tt + CONV_W - 1]
    xc = xc.reshape(tm, d)

    lam = lam_ref[...]
    log_sig_lam = -_softplus(-lam)
    for n in range(LRU_BLOCKS):
        sl = slice(n * LRU_BLOCK_W, (n + 1) * LRU_BLOCK_W)
        xb = xc[:, sl]
        gates = _dot(xb.astype(BF16), gw_ref[n]) + gb_ref[n:n + 1, :]
        r = jax.nn.sigmoid(gates[:, :LRU_BLOCK_W])
        ig = jax.nn.sigmoid(gates[:, LRU_BLOCK_W:])
        log_a = RG_C * r * log_sig_lam[:, sl]
        a_scr[:, sl] = jnp.exp(log_a)
        th = jnp.tanh(log_a)
        u_scr[:, sl] = jnp.sqrt(-2.0 * th / (1.0 - th)) * (ig * xb)

    cwid = min(d, SCAN_CARRY_VREGS * SUBLANES * LANES // bsz)
    for c in range(d // cwid):
        ls = slice(c * cwid, (c + 1) * cwid)

        def step(t, hc, ls=ls):
            rows = pl.ds(pl.multiple_of(t * bsz, bsz), bsz)
            hc = a_scr[rows, ls] * hc + u_scr[rows, ls]
            u_scr[rows, ls] = hc
            return hc

        hlast_ref[:, ls] = lax.fori_loop(0, tt, step, hlast_ref[:, ls])

    gelu = 0.5 * gate_branch * (1.0 + jnp.tanh(
        0.7978845608028654 * (gate_branch + 0.044715 * (gate_branch * gate_branch * gate_branch))))
    y = _dot((u_scr[...] * gelu).astype(BF16), wout_ref[...])
    o_ref[...] = x + (1.0 + g_ref[...])[None] * y.reshape(tt, bsz, d)


def _lru(x, mods, layer, h0, conv0, w_in, conv_w, conv_b, gate_w, gate_b, lam, w_out):
    t_len, bsz, d = x.shape
    tt = _time_tile(t_len, bsz)
    xspec = pl.BlockSpec((tt, bsz, d), lambda i: (i, 0, 0))

    def full(a):
        return _resident(a.shape, lambda i, n=a.ndim: (0,) * n)

    args = (h0, conv0, w_in, conv_w, conv_b, gate_w, gate_b, lam, w_out)
    return pl.pallas_call(
        _lru_kernel,
        grid=(t_len // tt,),
        in_specs=[xspec, _mod_spec(bsz, layer, 3), _mod_spec(bsz, layer, 4), _mod_spec(bsz, layer, 5)]
        + [full(a) for a in args],
        out_specs=[xspec,
                   pl.BlockSpec((bsz, d), lambda i: (0, 0)),
                   pl.BlockSpec((CONV_W - 1, bsz, d), lambda i: (0, 0, 0))],
        out_shape=[jax.ShapeDtypeStruct(x.shape, F32),
                   jax.ShapeDtypeStruct((bsz, d), F32),
                   jax.ShapeDtypeStruct((CONV_W - 1, bsz, d), F32)],
        scratch_shapes=[pltpu.VMEM((tt * bsz, d), F32), pltpu.VMEM((tt * bsz, d), F32)],
        compiler_params=_params("arbitrary"),
        name="rglru",
    )(x, mods, mods, mods, *args)


def _rwkv_in_kernel(*refs, has_vres):
    (x_ref, sh_ref, sc_ref, shift0_ref, mu_ref, wrkv_ref, w0_ref, w1_ref, w2_ref, a0_ref, a1_ref,
     a2_ref, g1_ref, g2_ref, kk_ref, ka_ref) = refs[:16]
    refs = refs[16:]
    if has_vres:
        v0_ref, v1_ref, v2_ref, vfirst_ref = refs[:4]
        refs = refs[4:]
    r_out, w_out, k_out, v_out, na_out, b_out, g_out, shift_ref = refs

    @pl.when(pl.program_id(0) == 0)
    def _():
        shift_ref[...] = shift0_ref[...]

    x = x_ref[...]
    tt, bsz, d = x.shape
    tm = tt * bsz
    xf = _rms_mod(x, sh_ref[...], sc_ref[...])
    prev = jnp.concatenate([shift_ref[...][None], xf[:tt - 1]], axis=0)
    shift_ref[...] = xf[tt - 1]
    dx = prev - xf

    def mix(j):
        return (xf + dx * mu_ref[j:j + 1, :][None]).reshape(tm, d).astype(BF16)

    r = _dot(mix(0), wrkv_ref[0])
    k = _dot(mix(1), wrkv_ref[1])
    xm_v = mix(2)
    v = _dot(xm_v, wrkv_ref[2])
    z = w0_ref[...] + _dot(jnp.tanh(_dot(mix(3), w1_ref[...])).astype(BF16), w2_ref[...])
    w_log = -_softplus(-z) - 0.5
    decay = jnp.exp(-jnp.exp(w_log))
    if has_vres:
        mix_v = jax.nn.sigmoid(v0_ref[...] + _dot(_dot(xm_v, v1_ref[...]).astype(BF16), v2_ref[...]))
        v = v + (vfirst_ref[...].reshape(tm, d) - v) * mix_v
    a = jax.nn.sigmoid(a0_ref[...] + _dot(_dot(mix(4), a1_ref[...]).astype(BF16), a2_ref[...]))
    g = _dot(jax.nn.sigmoid(_dot(mix(5), g1_ref[...])).astype(BF16), g2_ref[...])

    kk = k * kk_ref[...]
    norm = jnp.sqrt(_head_sum(kk * kk, _head_ones()))
    kk = kk / jnp.maximum(norm, 1e-12)
    k = k * (1.0 + (a - 1.0) * ka_ref[...])

    shp = (tt, bsz, d)
    r_out[...] = r.reshape(shp)
    w_out[...] = decay.reshape(shp)
    k_out[...] = k.reshape(shp)
    v_out[...] = v.reshape(shp)
    na_out[...] = (-kk).reshape(shp)
    b_out[...] = (kk * a).reshape(shp)
    g_out[...] = g.reshape(shp)


def _rwkv_in(x, mods, layer, shift0, weights, vres):
    t_len, bsz, d = x.shape
    tt = _time_tile(t_len, bsz, RWKV_IN_ROW_TILE)
    xspec = pl.BlockSpec((tt, bsz, d), lambda i: (i, 0, 0))

    def full(a):
        return _resident(a.shape, lambda i, n=a.ndim: (0,) * n)

    args = [shift0] + list(weights)
    in_specs = [xspec, _mod_spec(bsz, layer, 3), _mod_spec(bsz, layer, 4)] + [full(a) for a in args]
    if vres is not None:
        v0, v1, v2, v_first = vres
        args += [v0, v1, v2, v_first]
        in_specs += [full(v0), full(v1), full(v2), xspec]
    act = jax.ShapeDtypeStruct(x.shape, F32)
    return pl.pallas_call(
        functools.partial(_rwkv_in_kernel, has_vres=vres is not None),
        grid=(t_len // tt,),
        in_specs=in_specs,
        out_specs=[xspec] * 7 + [pl.BlockSpec((bsz, d), lambda i: (0, 0))],
        out_shape=[act] * 7 + [jax.ShapeDtypeStruct((bsz, d), F32)],
        compiler_params=_params("arbitrary"),
        name="rwkv_in",
    )(x, mods, mods, *args)


def _wkv_kernel(*refs, has_state):
    if has_state:
        r_ref, w_ref, k_ref, v_ref, a_ref, b_ref, s0_ref, y_ref, s_out_ref, s_scr = refs
    else:
        r_ref, w_ref, k_ref, v_ref, a_ref, b_ref, y_ref, s_out_ref, s_scr = refs
    tb, bb = r_ref.shape[0], r_ref.shape[1]

    @pl.when(pl.program_id(1) == 0)
    def _():
        if has_state:
            def pack(b, carry):
                for p in range(HEAD_PAIRS):
                    s_scr[b, p] = jnp.concatenate([s0_ref[b, 2 * p], s0_ref[b, 2 * p + 1]], axis=1)
                return carry
            lax.fori_loop(0, bb, pack, 0)
        else:
            s_scr[...] = jnp.zeros_like(s_scr)

    s_ref = s_scr
    head0 = lax.broadcasted_iota(jnp.int32, (SUBLANES, LANES), 1) < HEAD_SIZE
    rows_s = lax.broadcasted_iota(jnp.int32, (WKV_STATE_COLS, LANES), 0)
    rows_y = lax.broadcasted_iota(jnp.int32, (WKV_COLS, 2 * LANES), 0)
    seq_half = lax.broadcasted_iota(jnp.int32, (WKV_COLS, 2 * LANES), 1) // LANES
    zero8 = jnp.zeros((SUBLANES, LANES), F32)
    zero_rows = jnp.zeros((WKV_STATE_COLS, HEAD_SIZE), F32)
    contract_lanes = (((1,), (1,)), ((), ()))

    def split(x):
        return [jnp.where(head0, x, 0.0), jnp.where(head0, 0.0, x)]

    def head_dot(x):
        s0 = jnp.sum(jnp.where(head0, x, 0.0), axis=1, keepdims=True)
        s1 = jnp.sum(jnp.where(head0, 0.0, x), axis=1, keepdims=True)
        return jnp.where(head0, s0, s1)

    def stage_a(t, b0, p):
        idx = (t, pl.ds(b0, SUBLANES), pl.ds(p * LANES, LANES))
        rt, wt, at = (ref[idx] for ref in (r_ref, w_ref, a_ref))
        lmat = jnp.concatenate(split(at) + split(wt * rt), axis=0).astype(BF16)
        none = jnp.zeros_like(lmat)
        proj = None
        for i in range(SUBLANES):
            l_i = jnp.where(rows_s % SUBLANES == i, lmat, none)
            d = lax.dot_general(s_ref[b0 + i, p].astype(BF16), l_i, contract_lanes,
                                preferred_element_type=F32)
            proj = d if proj is None else proj + d
        return proj

    def stage_b(t, b0, p, proj):
        idx = (t, pl.ds(b0, SUBLANES), pl.ds(p * LANES, LANES))
        rt, wt, kt, vt, bt = (ref[idx] for ref in (r_ref, w_ref, k_ref, v_ref, b_ref))
        v_rows = jnp.concatenate([zero_rows, vt[:, :HEAD_SIZE], vt[:, HEAD_SIZE:]], axis=0)
        pfull = jnp.concatenate([proj, v_rows.T[:, WKV_STATE_COLS:]], axis=1)
        pb = pfull.astype(BF16)
        ymat = jnp.concatenate(split(bt) + [zero8, zero8] + split(kt), axis=0).astype(BF16)
        ymat = jnp.concatenate([ymat, ymat], axis=1)
        none = jnp.zeros_like(ymat)
        for i in range(0, SUBLANES, 2):
            upd = _dot(pb, jnp.where(rows_y % SUBLANES == i + seq_half, ymat, none))
            s_ref[b0 + i, p] = s_ref[b0 + i, p] * wt[i:i + 1] + upd[:, :LANES]
            s_ref[b0 + i + 1, p] = s_ref[b0 + i + 1, p] * wt[i + 1:i + 2] + upd[:, LANES:]
        pt = proj.T
        sa = jnp.concatenate([pt[0:SUBLANES], pt[SUBLANES:2 * SUBLANES]], axis=1)
        swr = jnp.concatenate([pt[2 * SUBLANES:3 * SUBLANES], pt[3 * SUBLANES:4 * SUBLANES]], axis=1)
        y_ref[idx] = swr + sa * head_dot(bt * rt) + vt * head_dot(kt * rt)

    tiles = [(g * SUBLANES, p) for g in range(bb // SUBLANES) for p in range(HEAD_PAIRS)]
    unroll = max(u for u in range(1, WKV_TIME_UNROLL + 1) if tb % u == 0)

    def t_body(tu, carry):
        work = [(tu * unroll + u, b0, p) for u in range(unroll) for (b0, p) in tiles]
        pending = {}
        for step in range(len(work) + WKV_LAG):
            if step < len(work):
                pending[step] = stage_a(*work[step])
            if step >= WKV_LAG:
                stage_b(*work[step - WKV_LAG], pending.pop(step - WKV_LAG))
        return carry

    lax.fori_loop(0, tb // unroll, t_body, 0)

    @pl.when(pl.program_id(1) == pl.num_programs(1) - 1)
    def _():
        def unpack(b, carry):
            for p in range(HEAD_PAIRS):
                s = s_scr[b, p]
                s_out_ref[b, 2 * p] = s[:, :HEAD_SIZE]
                s_out_ref[b, 2 * p + 1] = s[:, HEAD_SIZE:]
            return carry
        lax.fori_loop(0, bb, unpack, 0)


def _wkv(r, w, k, v, a, b, s0):
    t_len, bsz, d = r.shape
    bb = min(bsz, WKV_BATCH_TILE)
    tb = min(t_len, WKV_TIME_TILE)
    xspec = pl.BlockSpec((tb, bb, d), lambda i, j: (j, i, 0))
    sspec = pl.BlockSpec((bb, RWKV_HEADS, HEAD_SIZE, HEAD_SIZE), lambda i, j: (i, 0, 0, 0))
    has_state = s0 is not None
    return pl.pallas_call(
        functools.partial(_wkv_kernel, has_state=has_state),
        grid=(bsz // bb, t_len // tb),
        in_specs=[xspec] * 6 + ([sspec] if has_state else []),
        out_specs=[xspec, sspec],
        out_shape=[jax.ShapeDtypeStruct(r.shape, F32),
                   jax.ShapeDtypeStruct((bsz, RWKV_HEADS, HEAD_SIZE, HEAD_SIZE), F32)],
        scratch_shapes=[pltpu.VMEM((bb, HEAD_PAIRS, HEAD_SIZE, LANES), F32)],
        compiler_params=_params("parallel", "arbitrary"),
        name="wkv",
    )(r, w, k, v, a, b, *([s0] if has_state else []))


def _rwkv_out_kernel(y_ref, r_ref, k_ref, v_ref, g_ref, x_ref, gate_ref, lnw_ref, lnb_ref, rk_ref,
                     wo_ref, o_ref):
    x = x_ref[...]
    tt, bsz, d = x.shape
    tm = tt * bsz
    ones = _head_ones()
    y = y_ref[...].reshape(tm, d)
    yc = y - _head_sum(y, ones) * (1.0 / HEAD_SIZE)
    var = _head_sum(yc * yc, ones) * (1.0 / HEAD_SIZE)
    yn = yc * lax.rsqrt(var + GN_EPS) * lnw_ref[...] + lnb_ref[...]
    rk = r_ref[...].reshape(tm, d) * k_ref[...].reshape(tm, d) * rk_ref[...]
    bonus = _head_sum(rk, ones) * v_ref[...].reshape(tm, d)
    out = _dot(((yn + bonus) * g_ref[...].reshape(tm, d)).astype(BF16), wo_ref[...])
    o_ref[...] = x + (1.0 + gate_ref[...])[None] * out.reshape(tt, bsz, d)


def _rwkv_out(y, r, k, v, g, x, mods, layer, ln_w, ln_b, r_k, w_o):
    t_len, bsz, d = x.shape
    tt = _time_tile(t_len, bsz)
    xspec = pl.BlockSpec((tt, bsz, d), lambda i: (i, 0, 0))

    def full(a):
        return _resident(a.shape, lambda i, n=a.ndim: (0,) * n)

    return pl.pallas_call(
        _rwkv_out_kernel,
        grid=(t_len // tt,),
        in_specs=[xspec] * 6 + [_mod_spec(bsz, layer, 5)] + [full(a) for a in (ln_w, ln_b, r_k, w_o)],
        out_specs=xspec,
        out_shape=jax.ShapeDtypeStruct(x.shape, F32),
        compiler_params=_params("parallel"),
        name="rwkv_out",
    )(y, r, k, v, g, x, mods, ln_w, ln_b, r_k, w_o)


def _final_kernel(x_ref, gain_ref, o_ref, *, batch_major_out):
    x = x_ref[...]
    ms = jnp.mean(x * x, axis=-1, keepdims=True)
    y = (x * lax.rsqrt(ms + NORM_EPS)) * gain_ref[...][None]
    o_ref[...] = jnp.swapaxes(y, 0, 1) if batch_major_out else y


def _batch_major_tiles(t_len, bsz):
    return _time_tile(t_len, bsz) % SUBLANES == 0


def _final(x, gain, batch_major_out):
    t_len, bsz, d = x.shape
    tt = _time_tile(t_len, bsz)
    xspec = pl.BlockSpec((tt, bsz, d), lambda i: (i, 0, 0))
    return pl.pallas_call(
        functools.partial(_final_kernel, batch_major_out=batch_major_out),
        grid=(t_len // tt,),
        in_specs=[xspec, pl.BlockSpec((1, d), lambda i: (0, 0))],
        out_specs=pl.BlockSpec((bsz, tt, d), lambda i: (0, i, 0)) if batch_major_out else xspec,
        out_shape=jax.ShapeDtypeStruct((bsz, t_len, d) if batch_major_out else x.shape, F32),
        compiler_params=_params("parallel"),
        name="final_norm",
    )(x, gain)


def _pad_cols(w):
    n = w.shape[-1]
    return jnp.pad(w, [(0, 0)] * (w.ndim - 1) + [(0, -n % LORA_PAD)])


def _pad_rows(w):
    n = w.shape[-2]
    return jnp.pad(w, [(0, 0)] * (w.ndim - 2) + [(0, -n % LORA_PAD), (0, 0)])


def _trunk(x, mods, lru_h0, lru_conv0, rwkv_shift0, rwkv_wkv0, W):
    new_h, new_conv, new_shift, new_wkv = [], [], [], []
    v_first = None
    fold = _batch_major_tiles(x.shape[1], x.shape[0])
    if not fold:
        x = jnp.transpose(x, (1, 0, 2))
    for layer in range(DEPTH):
        j = layer // 2
        x = _ffn(x, mods, layer, 0, W['ffn_w_in'], W['ffn_w_out'], batch_major_in=fold and layer == 0)
        if layer % 2 == 0:
            x, h_last, buf = _lru(x, mods, layer, lru_h0[j], lru_conv0[j], W['lru_w_in'][j],
                                  W['lru_conv_w'][j], W['lru_conv_b'][j], W['lru_gate_w'][j],
                                  W['lru_gate_b'][j], W['lru_lambda'][j], W['lru_w_out'][j])
            new_h.append(h_last)
            new_conv.append(buf)
        else:
            weights = [W[n][j] for n in ('rwkv_mu', 'rwkv_w_rkv', 'rwkv_w0', 'rwkv_w1', 'rwkv_w2',
                                         'rwkv_a0', 'rwkv_a1', 'rwkv_a2', 'rwkv_g1', 'rwkv_g2',
                                         'rwkv_k_k', 'rwkv_k_a')]
            vres = None if j == 0 else (W['rwkv_v0'][j - 1], W['rwkv_v1'][j - 1], W['rwkv_v2'][j - 1],
                                        v_first)
            r, w, k, v, na, b, g, shift_last = _rwkv_in(x, mods, layer, rwkv_shift0[j], weights, vres)
            if v_first is None:
                v_first = v
            y, s_last = _wkv(r, w, k, v, na, b, None if rwkv_wkv0 is None else rwkv_wkv0[j])
            x = _rwkv_out(y, r, k, v, g, x, mods, layer, W['rwkv_ln_w'][j], W['rwkv_ln_b'][j],
                          W['rwkv_r_k'][j], W['rwkv_w_o'][j])
            new_shift.append(shift_last)
            new_wkv.append(s_last)
        x = _ffn(x, mods, layer, 1, W['ffn_w_in'], W['ffn_w_out'])
    y = _final(x, W['final_gain'], batch_major_out=fold)
    if not fold:
        y = jnp.transpose(y, (1, 0, 2))
    return y, jnp.stack(new_h), jnp.stack(new_conv), jnp.stack(new_shift), jnp.stack(new_wkv)


def _prep_weights(P):
    row = lambda a: a.reshape(a.shape[0], 1, -1)
    W = {
        'ffn_w_in': P['ffn_w_in'].astype(BF16),
        'ffn_w_out': P['ffn_w_out'].astype(BF16),
        'lru_w_in': P['lru_w_in'].astype(BF16),
        'lru_conv_w': P['lru_conv_w'],
        'lru_conv_b': row(P['lru_conv_b']),
        'lru_gate_w': P['lru_gate_w'].astype(BF16),
        'lru_gate_b': P['lru_gate_b'],
        'lru_lambda': row(P['lru_lambda']),
        'lru_w_out': P['lru_w_out'].astype(BF16),
        'rwkv_mu': P['rwkv_mu'],
        'rwkv_w_rkv': P['rwkv_w_rkv'].astype(BF16),
        'rwkv_w_o': P['rwkv_w_o'].astype(BF16),
        'rwkv_r_k': row(P['rwkv_r_k']),
        'final_gain': P['final_gain'].reshape(1, -1),
    }
    for n in ('rwkv_w0', 'rwkv_a0', 'rwkv_v0', 'rwkv_k_k', 'rwkv_k_a', 'rwkv_ln_w', 'rwkv_ln_b'):
        W[n] = row(P[n])
    for n in ('rwkv_w1', 'rwkv_a1', 'rwkv_v1', 'rwkv_g1'):
        W[n] = _pad_cols(P[n]).astype(BF16)
    for n in ('rwkv_w2', 'rwkv_a2', 'rwkv_v2', 'rwkv_g2'):
        W[n] = _pad_rows(P[n]).astype(BF16)
    return W


def _run_stream(x, mods, lru_h, lru_conv, rwkv_shift, rwkv_wkv, W):
    y, h, conv, shift, wkv = _trunk(x, mods, lru_h, jnp.transpose(lru_conv, (0, 2, 1, 3)), rwkv_shift,
                                    rwkv_wkv, W)
    return y, h, jnp.transpose(conv, (0, 2, 1, 3)), shift, wkv


def kernel(x_prompt, x_sample, state_lru_h, state_lru_conv, state_rwkv_shift, state_rwkv_wkv, c_prompt, c_sample, ada_w, ada_b, ffn_w_in, ffn_w_out, lru_w_in, lru_conv_w, lru_conv_b, lru_gate_w, lru_gate_b, lru_lambda, lru_w_out, rwkv_mu, rwkv_w_rkv, rwkv_w_o, rwkv_w0, rwkv_w1, rwkv_w2, rwkv_a0, rwkv_a1, rwkv_a2, rwkv_v0, rwkv_v1, rwkv_v2, rwkv_g1, rwkv_g2, rwkv_k_k, rwkv_k_a, rwkv_r_k, rwkv_ln_w, rwkv_ln_b, final_gain):
    P = dict(ffn_w_in=ffn_w_in, ffn_w_out=ffn_w_out,
             lru_w_in=lru_w_in, lru_conv_w=lru_conv_w, lru_conv_b=lru_conv_b, lru_gate_w=lru_gate_w,
             lru_gate_b=lru_gate_b, lru_lambda=lru_lambda, lru_w_out=lru_w_out,
             rwkv_mu=rwkv_mu, rwkv_w_rkv=rwkv_w_rkv, rwkv_w_o=rwkv_w_o, rwkv_w0=rwkv_w0, rwkv_w1=rwkv_w1,
             rwkv_w2=rwkv_w2, rwkv_a0=rwkv_a0, rwkv_a1=rwkv_a1, rwkv_a2=rwkv_a2, rwkv_v0=rwkv_v0,
             rwkv_v1=rwkv_v1, rwkv_v2=rwkv_v2, rwkv_g1=rwkv_g1, rwkv_g2=rwkv_g2, rwkv_k_k=rwkv_k_k,
             rwkv_k_a=rwkv_k_a, rwkv_r_k=rwkv_r_k, rwkv_ln_w=rwkv_ln_w, rwkv_ln_b=rwkv_ln_b,
             final_gain=final_gain)
    W = _prep_weights(P)
    bp, bs = x_prompt.shape[0], x_sample.shape[0]
    mods = _ada(jnp.concatenate([c_prompt, c_sample], axis=0), ada_w, ada_b)
    n_lru, n_rwkv = state_lru_h.shape[0], state_rwkv_shift.shape[0]
    zeros = lambda *s: jnp.zeros(s, F32)
    out_p = _run_stream(x_prompt, mods[:, :bp], zeros(n_lru, bp, D_RNN), zeros(n_lru, bp, CONV_W - 1, D_RNN),
                        zeros(n_rwkv, bp, D_MODEL), None, W)
    out_s = _run_stream(x_sample, mods[:, bp:], state_lru_h, state_lru_conv, state_rwkv_shift,
                        state_rwkv_wkv, W)
    return (out_p[0], out_s[0]) + out_p[1:] + out_s[1:]
```

```python
import functools

import jax
import jax.numpy as jnp
from jax import lax
from jax.experimental import pallas as pl
from jax.experimental.pallas import tpu as pltpu

D_MODEL = 1024
DEPTH = 4
D_RNN = D_MODEL
LRU_BLOCKS = 4
LRU_BLOCK_W = D_RNN // LRU_BLOCKS
CONV_W = 4
RG_C = 8.0
HEAD_SIZE = 64
RWKV_HEADS = D_MODEL // HEAD_SIZE
D_FF = 2816
FFN_RES = 0.5
N_MOD = 9
NORM_EPS = 1e-6
GN_EPS = HEAD_SIZE * 1e-5
DECAY_SCALE = 0.6065306597126334

LANES = 128
SUBLANES = 8
HEAD_PAIRS = D_MODEL // LANES
LORA_PAD = 128
VMEM_LIMIT_BYTES = 56 * 1024 * 1024
ROW_TILE = 512
RWKV_IN_ROW_TILE = 256
FFN_CHUNK = 256
ADA_TILE = 1152
SCAN_CARRY_VREGS = 16
WKV_STATE_COLS = 8 * SUBLANES
WKV_COLS = 12 * SUBLANES
WKV_BATCH_TILE = 8
WKV_TIME_TILE = 32
WKV_CHUNK_UNROLL = 4
WKV_LAG = 3

F32 = jnp.float32
BF16 = jnp.bfloat16


def _dot(a, b):
    return jnp.dot(a, b, preferred_element_type=F32)


def _params(*sem):
    return pltpu.CompilerParams(dimension_semantics=sem, vmem_limit_bytes=VMEM_LIMIT_BYTES)


def _resident(block_shape, index_map):
    return pl.BlockSpec(block_shape, index_map, pipeline_mode=pl.Buffered(1))


def _rms_mod(x, shift, scale):
    ms = jnp.mean(x * x, axis=-1, keepdims=True)
    return (x * lax.rsqrt(ms + NORM_EPS)) * (1.0 + scale)[None] + shift[None]


def _softplus(x):
    return jnp.maximum(x, 0.0) + jnp.log1p(jnp.exp(-jnp.abs(x)))


def _head_ones():
    r = lax.broadcasted_iota(jnp.int32, (LANES, LANES), 0) // HEAD_SIZE
    c = lax.broadcasted_iota(jnp.int32, (LANES, LANES), 1) // HEAD_SIZE
    return (r == c).astype(BF16)


def _head_sum(x, ones):
    outs = []
    for c in range(HEAD_PAIRS):
        xc = x[:, c * LANES:(c + 1) * LANES]
        hi = xc.astype(BF16)
        lo = (xc - hi.astype(F32)).astype(BF16)
        outs.append(_dot(hi, ones) + _dot(lo, ones))
    return jnp.concatenate(outs, axis=-1)


def _ada_kernel(c_ref, w_ref, b_ref, o_ref):
    c = c_ref[...]
    s = (c * jax.nn.sigmoid(c)).astype(BF16)
    o_ref[...] = _dot(s, w_ref[...].astype(BF16)) + b_ref[...]


def _ada(c_all, ada_w, ada_b):
    n = c_all.shape[0]
    nd = N_MOD * D_MODEL
    return pl.pallas_call(
        _ada_kernel,
        grid=(DEPTH, nd // ADA_TILE),
        in_specs=[
            pl.BlockSpec((n, D_MODEL), lambda l, j: (0, 0)),
            pl.BlockSpec((None, D_MODEL, ADA_TILE), lambda l, j: (l, 0, j)),
            pl.BlockSpec((None, 1, ADA_TILE), lambda l, j: (l, 0, j)),
        ],
        out_specs=pl.BlockSpec((None, n, ADA_TILE), lambda l, j: (l, 0, j)),
        out_shape=jax.ShapeDtypeStruct((DEPTH, n, nd), F32),
        compiler_params=_params("parallel", "parallel"),
        name="ada_mod",
    )(c_all, ada_w, ada_b.reshape(DEPTH, 1, nd))


def _mod_spec(bsz, layer, m):
    return pl.BlockSpec((None, bsz, D_MODEL), lambda *_: (layer, 0, m))


def _time_tile(t_len, bsz, rows=ROW_TILE):
    return max(1, min(t_len, rows // bsz))


def _ffn_kernel(x_ref, sh_ref, sc_ref, g_ref, win_ref, wout_ref, o_ref, *, batch_major_in):
    x = x_ref[...]
    if batch_major_in:
        x = jnp.swapaxes(x, 0, 1)
    tt, bsz, d = x.shape
    h = _rms_mod(x, sh_ref[...], sc_ref[...]).reshape(tt * bsz, d).astype(BF16)
    acc = jnp.zeros((tt * bsz, d), F32)
    for j in range(D_FF // FFN_CHUNK):
        lo = j * FFN_CHUNK
        gate = _dot(h, win_ref[:, lo:lo + FFN_CHUNK])
        up = _dot(h, win_ref[:, D_FF + lo:D_FF + lo + FFN_CHUNK])
        act = (gate * jax.nn.sigmoid(gate) * up).astype(BF16)
        acc = acc + _dot(act, wout_ref[lo:lo + FFN_CHUNK, :])
    o_ref[...] = x + (FFN_RES * (1.0 + g_ref[...]))[None] * acc.reshape(tt, bsz, d)


def _ffn(x, mods, layer, sub, w_in, w_out, batch_major_in=False):
    if batch_major_in:
        bsz, t_len, d = x.shape
    else:
        t_len, bsz, d = x.shape
    tt = _time_tile(t_len, bsz)
    m0 = 0 if sub == 0 else 6
    xspec = pl.BlockSpec((tt, bsz, d), lambda i: (i, 0, 0))
    return pl.pallas_call(
        functools.partial(_ffn_kernel, batch_major_in=batch_major_in),
        grid=(t_len // tt,),
        in_specs=[
            pl.BlockSpec((bsz, tt, d), lambda i: (0, i, 0)) if batch_major_in else xspec,
            _mod_spec(bsz, layer, m0), _mod_spec(bsz, layer, m0 + 1), _mod_spec(bsz, layer, m0 + 2),
            _resident((None, None, d, 2 * D_FF), lambda i: (layer, sub, 0, 0)),
            _resident((None, None, D_FF, d), lambda i: (layer, sub, 0, 0)),
        ],
        out_specs=xspec,
        out_shape=jax.ShapeDtypeStruct((t_len, bsz, d), F32),
        compiler_params=_params("parallel"),
        name="ffn",
    )(x, mods, mods, mods, w_in, w_out)


def _lru_kernel(x_ref, sh_ref, sc_ref, g_ref, h0_ref, conv0_ref, win_ref, cw_ref, cb_ref, gw_ref,
                gb_ref, lam_ref, wout_ref, o_ref, hlast_ref, buf_ref, a_scr, u_scr):
    @pl.when(pl.program_id(0) == 0)
    def _():
        hlast_ref[...] = h0_ref[...]
        buf_ref[...] = conv0_ref[...]

    x = x_ref[...]
    tt, bsz, d = x.shape
    tm = tt * bsz
    h = _rms_mod(x, sh_ref[...], sc_ref[...]).reshape(tm, d).astype(BF16)
    proj = _dot(h, win_ref[...])
    gate_branch = proj[:, :d]
    xp = jnp.concatenate([buf_ref[...], proj[:, d:].reshape(tt, bsz, d)], axis=0)
    cw = cw_ref[...]
    xc = cb_ref[...][None]
    for j in range(CONV_W):
        xc = xc + xp[j:j + tt] * cw[j:j + 1][None]
    buf_ref[...] = xp[tt:tt + CONV_W - 1]
    xc = xc.reshape(tm, d)

    lam = lam_ref[...]
    log_sig_lam = -_softplus(-lam)
    for n in range(LRU_BLOCKS):
        sl = slice(n * LRU_BLOCK_W, (n + 1) * LRU_BLOCK_W)
        xb = xc[:, sl]
        gates = _dot(xb.astype(BF16), gw_ref[n]) + gb_ref[n:n + 1, :]
        r = jax.nn.sigmoid(gates[:, :LRU_BLOCK_W])
        ig = jax.nn.sigmoid(gates[:, LRU_BLOCK_W:])
        log_a = RG_C * r * log_sig_lam[:, sl]
        a_scr[:, sl] = jnp.exp(log_a)
        th = jnp.tanh(log_a)
        u_scr[:, sl] = jnp.sqrt(-2.0 * th / (1.0 - th)) * (ig * xb)

    cwid = min(d, SCAN_CARRY_VREGS * SUBLANES * LANES // bsz)
    for c in range(d // cwid):
        ls = slice(c * cwid, (c + 1) * cwid)

        def step(t, hc, ls=ls):
            rows = pl.ds(pl.multiple_of(t * bsz, bsz), bsz)
            hc = a_scr[rows, ls] * hc + u_scr[rows, ls]
            u_scr[rows, ls] = hc
            return hc

        hlast_ref[:, ls] = lax.fori_loop(0, tt, step, hlast_ref[:, ls])

    gelu = 0.5 * gate_branch * (1.0 + jnp.tanh(
        0.7978845608028654 * (gate_branch + 0.044715 * (gate_branch * gate_branch * gate_branch))))
    y = _dot((u_scr[...] * gelu).astype(BF16), wout_ref[...])
    o_ref[...] = x + (1.0 + g_ref[...])[None] * y.reshape(tt, bsz, d)


def _lru(x, mods, layer, h0, conv0, w_in, conv_w, conv_b, gate_w, gate_b, lam, w_out):
    t_len, bsz, d = x.shape
    tt = _time_tile(t_len, bsz)
    xspec = pl.BlockSpec((tt, bsz, d), lambda i: (i, 0, 0))

    def full(a):
        return _resident(a.shape, lambda i, n=a.ndim: (0,) * n)

    args = (h0, conv0, w_in, conv_w, conv_b, gate_w, gate_b, lam, w_out)
    return pl.pallas_call(
        _lru_kernel,
        grid=(t_len // tt,),
        in_specs=[xspec, _mod_spec(bsz, layer, 3), _mod_spec(bsz, layer, 4), _mod_spec(bsz, layer, 5)]
        + [full(a) for a in args],
        out_specs=[xspec,
                   pl.BlockSpec((bsz, d), lambda i: (0, 0)),
                   pl.BlockSpec((CONV_W - 1, bsz, d), lambda i: (0, 0, 0))],
        out_shape=[jax.ShapeDtypeStruct(x.shape, F32),
                   jax.ShapeDtypeStruct((bsz, d), F32),
                   jax.ShapeDtypeStruct((CONV_W - 1, bsz, d), F32)],
        scratch_shapes=[pltpu.VMEM((tt * bsz, d), F32), pltpu.VMEM((tt * bsz, d), F32)],
        compiler_params=_params("arbitrary"),
        name="rglru",
    )(x, mods, mods, mods, *args)


def _rwkv_in_kernel(*refs, has_vres):
    (x_ref, sh_ref, sc_ref, shift0_ref, mu_ref, wrkv_ref, w0_ref, w1_ref, w2_ref, a0_ref, a1_ref,
     a2_ref, g1_ref, g2_ref, kk_ref, ka_ref) = refs[:16]
    refs = refs[16:]
    if has_vres:
        v0_ref, v1_ref, v2_ref, vfirst_ref = refs[:4]
        refs = refs[4:]
    r_out, w_out, k_out, v_out, na_out, b_out, g_out, shift_ref = refs

    @pl.when(pl.program_id(0) == 0)
    def _():
        shift_ref[...] = shift0_ref[...]

    x = x_ref[...]
    tt, bsz, d = x.shape
    tm = tt * bsz
    xf = _rms_mod(x, sh_ref[...], sc_ref[...])
    prev = jnp.concatenate([shift_ref[...][None], xf[:tt - 1]], axis=0)
    shift_ref[...] = xf[tt - 1]
    dx = prev - xf

    def mix(j):
        return (xf + dx * mu_ref[j:j + 1, :][None]).reshape(tm, d).astype(BF16)

    r = _dot(mix(0), wrkv_ref[0])
    k = _dot(mix(1), wrkv_ref[1])
    xm_v = mix(2)
    v = _dot(xm_v, wrkv_ref[2])
    z = w0_ref[...] + _dot(jnp.tanh(_dot(mix(3), w1_ref[...])).astype(BF16), w2_ref[...])
    decay = jnp.exp(-DECAY_SCALE * jax.nn.sigmoid(z))
    if has_vres:
        mix_v = jax.nn.sigmoid(v0_ref[...] + _dot(_dot(xm_v, v1_ref[...]).astype(BF16), v2_ref[...]))
        v = v + (vfirst_ref[...].reshape(tm, d) - v) * mix_v
    a = jax.nn.sigmoid(a0_ref[...] + _dot(_dot(mix(4), a1_ref[...]).astype(BF16), a2_ref[...]))
    g = _dot(jax.nn.sigmoid(_dot(mix(5), g1_ref[...])).astype(BF16), g2_ref[...])

    kk = k * kk_ref[...]
    norm = jnp.sqrt(_head_sum(kk * kk, _head_ones()))
    kk = kk / jnp.maximum(norm, 1e-12)
    k = k * (1.0 + (a - 1.0) * ka_ref[...])

    shp = (tt, bsz, d)
    r_out[...] = r.reshape(shp)
    w_out[...] = decay.reshape(shp)
    k_out[...] = k.reshape(shp)
    v_out[...] = v.reshape(shp)
    na_out[...] = (-kk).reshape(shp)
    b_out[...] = (kk * a).reshape(shp)
    g_out[...] = g.reshape(shp)


def _rwkv_in(x, mods, layer, shift0, weights, vres):
    t_len, bsz, d = x.shape
    tt = _time_tile(t_len, bsz, RWKV_IN_ROW_TILE)
    xspec = pl.BlockSpec((tt, bsz, d), lambda i: (i, 0, 0))

    def full(a):
        return _resident(a.shape, lambda i, n=a.ndim: (0,) * n)

    args = [shift0] + list(weights)
    in_specs = [xspec, _mod_spec(bsz, layer, 3), _mod_spec(bsz, layer, 4)] + [full(a) for a in args]
    if vres is not None:
        v0, v1, v2, v_first = vres
        args += [v0, v1, v2, v_first]
        in_specs += [full(v0), full(v1), full(v2), xspec]
    act = jax.ShapeDtypeStruct(x.shape, F32)
    return pl.pallas_call(
        functools.partial(_rwkv_in_kernel, has_vres=vres is not None),
        grid=(t_len // tt,),
        in_specs=in_specs,
        out_specs=[xspec] * 7 + [pl.BlockSpec((bsz, d), lambda i: (0, 0))],
        out_shape=[act] * 7 + [jax.ShapeDtypeStruct((bsz, d), F32)],
        compiler_params=_params("arbitrary"),
        name="rwkv_in",
    )(x, mods, mods, *args)


def _wkv_kernel(*refs, has_state):
    if has_state:
        r_ref, w_ref, k_ref, v_ref, a_ref, b_ref, s0_ref, y_ref, s_out_ref, s_scr = refs
    else:
        r_ref, w_ref, k_ref, v_ref, a_ref, b_ref, y_ref, s_out_ref, s_scr = refs
    tb, bb = r_ref.shape[0], r_ref.shape[1]

    @pl.when(pl.program_id(1) == 0)
    def _():
        if has_state:
            def pack(b, carry):
                for p in range(HEAD_PAIRS):
                    s_scr[b, p] = jnp.concatenate([s0_ref[b, 2 * p], s0_ref[b, 2 * p + 1]], axis=1)
                return carry
            lax.fori_loop(0, bb, pack, 0)
        else:
            s_scr[...] = jnp.zeros_like(s_scr)

    s_ref = s_scr
    head0 = lax.broadcasted_iota(jnp.int32, (SUBLANES, LANES), 1) < HEAD_SIZE
    rows_s = lax.broadcasted_iota(jnp.int32, (WKV_STATE_COLS, LANES), 0)
    rows_y = lax.broadcasted_iota(jnp.int32, (WKV_COLS, 2 * LANES), 0)
    seq_half = lax.broadcasted_iota(jnp.int32, (WKV_COLS, 2 * LANES), 1) // LANES
    zero8 = jnp.zeros((SUBLANES, LANES), F32)
    zero_rows = jnp.zeros((WKV_STATE_COLS, HEAD_SIZE), F32)
    contract_lanes = (((1,), (1,)), ((), ()))

    def split(x):
        return [jnp.where(head0, x, 0.0), jnp.where(head0, 0.0, x)]

    def head_dot(x):
        s0 = jnp.sum(jnp.where(head0, x, 0.0), axis=1, keepdims=True)
        s1 = jnp.sum(jnp.where(head0, 0.0, x), axis=1, keepdims=True)
        return jnp.where(head0, s0, s1)

    def halves(x):
        return [x[:, :HEAD_SIZE], x[:, HEAD_SIZE:]]

    def load(ref, t, b0, p):
        return ref[t, pl.ds(b0, SUBLANES), pl.ds(p * LANES, LANES)]

    def stage_a(c, b0, p):
        t1, t2 = 2 * c, 2 * c + 1
        w1 = load(w_ref, t1, b0, p)
        vectors = [load(a_ref, t1, b0, p), load(a_ref, t2, b0, p) * w1, w1 * load(r_ref, t1, b0, p),
                   w1 * load(w_ref, t2, b0, p) * load(r_ref, t2, b0, p)]
        lmat = jnp.concatenate([h for x in vectors for h in split(x)], axis=0).astype(BF16)
        none = jnp.zeros_like(lmat)
        proj = None
        for i in range(SUBLANES):
            l_i = jnp.where(rows_s % SUBLANES == i, lmat, none)
            d = lax.dot_general(s_ref[b0 + i, p].astype(BF16), l_i, contract_lanes,
                                preferred_element_type=F32)
            proj = d if proj is None else proj + d
        return proj

    def stage_b(c, b0, p, proj):
        t1, t2 = 2 * c, 2 * c + 1
        r1, w1, k1, v1, b1 = (load(ref, t1, b0, p) for ref in (r_ref, w_ref, k_ref, v_ref, b_ref))
        r2, w2, k2, v2, a2, b2 = (load(ref, t2, b0, p) for ref in (r_ref, w_ref, k_ref, v_ref, a_ref, b_ref))
        w12 = w1 * w2
        b1a2 = head_dot(b1 * a2)
        k1a2 = head_dot(k1 * a2)
        ymat = jnp.concatenate(split(b1 * w2 + b1a2 * b2) + split(b2) + [zero8] * 4
                               + split(k1 * w2 + k1a2 * b2) + split(k2), axis=0).astype(BF16)
        ymat = jnp.concatenate([ymat, ymat], axis=1)
        none = jnp.zeros_like(ymat)
        v_rows = jnp.concatenate([zero_rows] + halves(v1) + halves(v2), axis=0)
        pb = jnp.concatenate([proj, v_rows.T[:, WKV_STATE_COLS:]], axis=1).astype(BF16)
        for i in range(0, SUBLANES, 2):
            upd = _dot(pb, jnp.where(rows_y % SUBLANES == i + seq_half, ymat, none))
            s_ref[b0 + i, p] = s_ref[b0 + i, p] * w12[i:i + 1] + upd[:, :LANES]
            s_ref[b0 + i + 1, p] = s_ref[b0 + i + 1, p] * w12[i + 1:i + 2] + upd[:, LANES:]
        pt = proj.T

        def rows_of(j):
            lo = 2 * j * SUBLANES
            return jnp.concatenate([pt[lo:lo + SUBLANES], pt[lo + SUBLANES:lo + 2 * SUBLANES]], axis=1)

        u1, s0a2, s0r1, s0r2 = rows_of(0), rows_of(1), rows_of(2), rows_of(3)
        b2r2 = head_dot(b2 * r2)
        w2r2 = w2 * r2
        y_ref[t1, pl.ds(b0, SUBLANES), pl.ds(p * LANES, LANES)] = (
            s0r1 + u1 * head_dot(b1 * r1) + v1 * head_dot(k1 * r1))
        y_ref[t2, pl.ds(b0, SUBLANES), pl.ds(p * LANES, LANES)] = (
            s0r2 + u1 * (head_dot(b1 * w2r2) + b1a2 * b2r2) + v1 * (head_dot(k1 * w2r2) + k1a2 * b2r2)
            + s0a2 * b2r2 + v2 * head_dot(k2 * r2))

    tiles = [(g * SUBLANES, p) for g in range(bb // SUBLANES) for p in range(HEAD_PAIRS)]
    n_chunks = tb // 2
    unroll = max(u for u in range(1, WKV_CHUNK_UNROLL + 1) if n_chunks % u == 0)

    def t_body(cu, carry):
        work = [(cu * unroll + u, b0, p) for u in range(unroll) for (b0, p) in tiles]
        pending = {}
        for step in range(len(work) + WKV_LAG):
            if step < len(work):
                pending[step] = stage_a(*work[step])
            if step >= WKV_LAG:
                stage_b(*work[step - WKV_LAG], pending.pop(step - WKV_LAG))
        return carry

    lax.fori_loop(0, n_chunks // unroll, t_body, 0)

    @pl.when(pl.program_id(1) == pl.num_programs(1) - 1)
    def _():
        def unpack(b, carry):
            for p in range(HEAD_PAIRS):
                s = s_scr[b, p]
                s_out_ref[b, 2 * p] = s[:, :HEAD_SIZE]
                s_out_ref[b, 2 * p + 1] = s[:, HEAD_SIZE:]
            return carry
        lax.fori_loop(0, bb, unpack, 0)


def _wkv(r, w, k, v, a, b, s0):
    t_len, bsz, d = r.shape
    bb = min(bsz, WKV_BATCH_TILE)
    tb = min(t_len, WKV_TIME_TILE)
    assert tb % 2 == 0 and t_len % tb == 0 and bsz % bb == 0
    xspec = pl.BlockSpec((tb, bb, d), lambda i, j: (j, i, 0))
    sspec = pl.BlockSpec((bb, RWKV_HEADS, HEAD_SIZE, HEAD_SIZE), lambda i, j: (i, 0, 0, 0))
    has_state = s0 is not None
    return pl.pallas_call(
        functools.partial(_wkv_kernel, has_state=has_state),
        grid=(bsz // bb, t_len // tb),
        in_specs=[xspec] * 6 + ([sspec] if has_state else []),
        out_specs=[xspec, sspec],
        out_shape=[jax.ShapeDtypeStruct(r.shape, F32),
                   jax.ShapeDtypeStruct((bsz, RWKV_HEADS, HEAD_SIZE, HEAD_SIZE), F32)],
        scratch_shapes=[pltpu.VMEM((bb, HEAD_PAIRS, HEAD_SIZE, LANES), F32)],
        compiler_params=_params("parallel", "arbitrary"),
        name="wkv",
    )(r, w, k, v, a, b, *([s0] if has_state else []))


def _rwkv_out_kernel(y_ref, r_ref, k_ref, v_ref, g_ref, x_ref, gate_ref, lnw_ref, lnb_ref, rk_ref,
                     wo_ref, o_ref):
    x = x_ref[...]
    tt, bsz, d = x.shape
    tm = tt * bsz
    ones = _head_ones()
    y = y_ref[...].reshape(tm, d)
    yc = y - _head_sum(y, ones) * (1.0 / HEAD_SIZE)
    var = _head_sum(yc * yc, ones) * (1.0 / HEAD_SIZE)
    yn = yc * lax.rsqrt(var + GN_EPS) * lnw_ref[...] + lnb_ref[...]
    rk = r_ref[...].reshape(tm, d) * k_ref[...].reshape(tm, d) * rk_ref[...]
    bonus = _head_sum(rk, ones) * v_ref[...].reshape(tm, d)
    out = _dot(((yn + bonus) * g_ref[...].reshape(tm, d)).astype(BF16), wo_ref[...])
    o_ref[...] = x + (1.0 + gate_ref[...])[None] * out.reshape(tt, bsz, d)


def _rwkv_out(y, r, k, v, g, x, mods, layer, ln_w, ln_b, r_k, w_o):
    t_len, bsz, d = x.shape
    tt = _time_tile(t_len, bsz)
    xspec = pl.BlockSpec((tt, bsz, d), lambda i: (i, 0, 0))

    def full(a):
        return _resident(a.shape, lambda i, n=a.ndim: (0,) * n)

    return pl.pallas_call(
        _rwkv_out_kernel,
        grid=(t_len // tt,),
        in_specs=[xspec] * 6 + [_mod_spec(bsz, layer, 5)] + [full(a) for a in (ln_w, ln_b, r_k, w_o)],
        out_specs=xspec,
        out_shape=jax.ShapeDtypeStruct(x.shape, F32),
        compiler_params=_params("parallel"),
        name="rwkv_out",
    )(y, r, k, v, g, x, mods, ln_w, ln_b, r_k, w_o)


def _final_kernel(x_ref, gain_ref, o_ref, *, batch_major_out):
    x = x_ref[...]
    ms = jnp.mean(x * x, axis=-1, keepdims=True)
    y = (x * lax.rsqrt(ms + NORM_EPS)) * gain_ref[...][None]
    o_ref[...] = jnp.swapaxes(y, 0, 1) if batch_major_out else y


def _batch_major_tiles(t_len, bsz):
    return _time_tile(t_len, bsz) % SUBLANES == 0


def _final(x, gain, batch_major_out):
    t_len, bsz, d = x.shape
    tt = _time_tile(t_len, bsz)
    xspec = pl.BlockSpec((tt, bsz, d), lambda i: (i, 0, 0))
    return pl.pallas_call(
        functools.partial(_final_kernel, batch_major_out=batch_major_out),
        grid=(t_len // tt,),
        in_specs=[xspec, pl.BlockSpec((1, d), lambda i: (0, 0))],
        out_specs=pl.BlockSpec((bsz, tt, d), lambda i: (0, i, 0)) if batch_major_out else xspec,
        out_shape=jax.ShapeDtypeStruct((bsz, t_len, d) if batch_major_out else x.shape, F32),
        compiler_params=_params("parallel"),
        name="final_norm",
    )(x, gain)


def _pad_cols(w):
    n = w.shape[-1]
    return jnp.pad(w, [(0, 0)] * (w.ndim - 1) + [(0, -n % LORA_PAD)])


def _pad_rows(w):
    n = w.shape[-2]
    return jnp.pad(w, [(0, 0)] * (w.ndim - 2) + [(0, -n % LORA_PAD), (0, 0)])


def _trunk(x, mods, lru_h0, lru_conv0, rwkv_shift0, rwkv_wkv0, W):
    new_h, new_conv, new_shift, new_wkv = [], [], [], []
    v_first = None
    fold = _batch_major_tiles(x.shape[1], x.shape[0])
    if not fold:
        x = jnp.transpose(x, (1, 0, 2))
    for layer in range(DEPTH):
        j = layer // 2
        x = _ffn(x, mods, layer, 0, W['ffn_w_in'], W['ffn_w_out'], batch_major_in=fold and layer == 0)
        if layer % 2 == 0:
            x, h_last, buf = _lru(x, mods, layer, lru_h0[j], lru_conv0[j], W['lru_w_in'][j],
                                  W['lru_conv_w'][j], W['lru_conv_b'][j], W['lru_gate_w'][j],
                                  W['lru_gate_b'][j], W['lru_lambda'][j], W['lru_w_out'][j])
            new_h.append(h_last)
            new_conv.append(buf)
        else:
            weights = [W[n][j] for n in ('rwkv_mu', 'rwkv_w_rkv', 'rwkv_w0', 'rwkv_w1', 'rwkv_w2',
                                         'rwkv_a0', 'rwkv_a1', 'rwkv_a2', 'rwkv_g1', 'rwkv_g2',
                                         'rwkv_k_k', 'rwkv_k_a')]
            vres = None if j == 0 else (W['rwkv_v0'][j - 1], W['rwkv_v1'][j - 1], W['rwkv_v2'][j - 1],
                                        v_first)
            r, w, k, v, na, b, g, shift_last = _rwkv_in(x, mods, layer, rwkv_shift0[j], weights, vres)
            if v_first is None:
                v_first = v
            y, s_last = _wkv(r, w, k, v, na, b, None if rwkv_wkv0 is None else rwkv_wkv0[j])
            x = _rwkv_out(y, r, k, v, g, x, mods, layer, W['rwkv_ln_w'][j], W['rwkv_ln_b'][j],
                          W['rwkv_r_k'][j], W['rwkv_w_o'][j])
            new_shift.append(shift_last)
            new_wkv.append(s_last)
        x = _ffn(x, mods, layer, 1, W['ffn_w_in'], W['ffn_w_out'])
    y = _final(x, W['final_gain'], batch_major_out=fold)
    if not fold:
        y = jnp.transpose(y, (1, 0, 2))
    return y, jnp.stack(new_h), jnp.stack(new_conv), jnp.stack(new_shift), jnp.stack(new_wkv)


def _prep_weights(P):
    row = lambda a: a.reshape(a.shape[0], 1, -1)
    W = {
        'ffn_w_in': P['ffn_w_in'].astype(BF16),
        'ffn_w_out': P['ffn_w_out'].astype(BF16),
        'lru_w_in': P['lru_w_in'].astype(BF16),
        'lru_conv_w': P['lru_conv_w'],
        'lru_conv_b': row(P['lru_conv_b']),
        'lru_gate_w': P['lru_gate_w'].astype(BF16),
        'lru_gate_b': P['lru_gate_b'],
        'lru_lambda': row(P['lru_lambda']),
        'lru_w_out': P['lru_w_out'].astype(BF16),
        'rwkv_mu': P['rwkv_mu'],
        'rwkv_w_rkv': P['rwkv_w_rkv'].astype(BF16),
        'rwkv_w_o': P['rwkv_w_o'].astype(BF16),
        'rwkv_r_k': row(P['rwkv_r_k']),
        'final_gain': P['final_gain'].reshape(1, -1),
    }
    for n in ('rwkv_w0', 'rwkv_a0', 'rwkv_v0', 'rwkv_k_k', 'rwkv_k_a', 'rwkv_ln_w', 'rwkv_ln_b'):
        W[n] = row(P[n])
    for n in ('rwkv_w1', 'rwkv_a1', 'rwkv_v1', 'rwkv_g1'):
        W[n] = _pad_cols(P[n]).astype(BF16)
    for n in ('rwkv_w2', 'rwkv_a2', 'rwkv_v2', 'rwkv_g2'):
        W[n] = _pad_rows(P[n]).astype(BF16)
    return W


def _run_stream(x, mods, lru_h, lru_conv, rwkv_shift, rwkv_wkv, W):
    y, h, conv, shift, wkv = _trunk(x, mods, lru_h, jnp.transpose(lru_conv, (0, 2, 1, 3)), rwkv_shift,
                                    rwkv_wkv, W)
    return y, h, jnp.transpose(conv, (0, 2, 1, 3)), shift, wkv


def kernel(x_prompt, x_sample, state_lru_h, state_lru_conv, state_rwkv_shift, state_rwkv_wkv, c_prompt, c_sample, ada_w, ada_b, ffn_w_in, ffn_w_out, lru_w_in, lru_conv_w, lru_conv_b, lru_gate_w, lru_gate_b, lru_lambda, lru_w_out, rwkv_mu, rwkv_w_rkv, rwkv_w_o, rwkv_w0, rwkv_w1, rwkv_w2, rwkv_a0, rwkv_a1, rwkv_a2, rwkv_v0, rwkv_v1, rwkv_v2, rwkv_g1, rwkv_g2, rwkv_k_k, rwkv_k_a, rwkv_r_k, rwkv_ln_w, rwkv_ln_b, final_gain):
    P = dict(ffn_w_in=ffn_w_in, ffn_w_out=ffn_w_out,
             lru_w_in=lru_w_in, lru_conv_w=lru_conv_w, lru_conv_b=lru_conv_b, lru_gate_w=lru_gate_w,
             lru_gate_b=lru_gate_b, lru_lambda=lru_lambda, lru_w_out=lru_w_out,
             rwkv_mu=rwkv_mu, rwkv_w_rkv=rwkv_w_rkv, rwkv_w_o=rwkv_w_o, rwkv_w0=rwkv_w0, rwkv_w1=rwkv_w1,
             rwkv_w2=rwkv_w2, rwkv_a0=rwkv_a0, rwkv_a1=rwkv_a1, rwkv_a2=rwkv_a2, rwkv_v0=rwkv_v0,
             rwkv_v1=rwkv_v1, rwkv_v2=rwkv_v2, rwkv_g1=rwkv_g1, rwkv_g2=rwkv_g2, rwkv_k_k=rwkv_k_k,
             rwkv_k_a=rwkv_k_a, rwkv_r_k=rwkv_r_k, rwkv_ln_w=rwkv_ln_w, rwkv_ln_b=rwkv_ln_b,
             final_gain=final_gain)
    W = _prep_weights(P)
    bp, bs = x_prompt.shape[0], x_sample.shape[0]
    mods = _ada(jnp.concatenate([c_prompt, c_sample], axis=0), ada_w, ada_b)
    n_lru, n_rwkv = state_lru_h.shape[0], state_rwkv_shift.shape[0]
    zeros = lambda *s: jnp.zeros(s, F32)
    out_p = _run_stream(x_prompt, mods[:, :bp], zeros(n_lru, bp, D_RNN), zeros(n_lru, bp, CONV_W - 1, D_RNN),
                        zeros(n_rwkv, bp, D_MODEL), None, W)
    out_s = _run_stream(x_sample, mods[:, bp:], state_lru_h, state_lru_conv, state_rwkv_shift,
                        state_rwkv_wkv, W)
    return (out_p[0], out_s[0]) + out_p[1:] + out_s[1:]
```

```python
import functools

import jax
import jax.numpy as jnp
from jax import lax
from jax.experimental import pallas as pl
from jax.experimental.pallas import tpu as pltpu

D_MODEL = 1024
DEPTH = 4
D_RNN = D_MODEL
LRU_BLOCKS = 4
LRU_BLOCK_W = D_RNN // LRU_BLOCKS
CONV_W = 4
RG_C = 8.0
HEAD_SIZE = 64
RWKV_HEADS = D_MODEL // HEAD_SIZE
D_FF = 2816
FFN_RES = 0.5
N_MOD = 9
NORM_EPS = 1e-6
GN_EPS = HEAD_SIZE * 1e-5
DECAY_SCALE = 0.6065306597126334

LANES = 128
SUBLANES = 8
HEAD_PAIRS = D_MODEL // LANES
LORA_PAD = 128
VMEM_LIMIT_BYTES = 56 * 1024 * 1024
ROW_TILE = 512
RWKV_IN_ROW_TILE = 256
FFN_CHUNK = 256
ADA_TILE = 1152
SCAN_CARRY_VREGS = 16
WKV_STATE_COLS = 8 * SUBLANES
WKV_COLS = 12 * SUBLANES
WKV_BATCH_TILE = 8
WKV_TIME_TILE = 32
WKV_CHUNK_UNROLL = 4
WKV_LAG = 3

F32 = jnp.float32
BF16 = jnp.bfloat16


def _dot(a, b):
    return jnp.dot(a, b, preferred_element_type=F32)


def _params(*sem):
    return pltpu.CompilerParams(dimension_semantics=sem, vmem_limit_bytes=VMEM_LIMIT_BYTES)


def _resident(block_shape, index_map):
    return pl.BlockSpec(block_shape, index_map, pipeline_mode=pl.Buffered(1))


def _rms_mod(x, shift, scale):
    ms = jnp.mean(x * x, axis=-1, keepdims=True)
    return (x * lax.rsqrt(ms + NORM_EPS)) * (1.0 + scale)[None] + shift[None]


def _softplus(x):
    return jnp.maximum(x, 0.0) + jnp.log1p(jnp.exp(-jnp.abs(x)))


def _head_ones():
    r = lax.broadcasted_iota(jnp.int32, (LANES, LANES), 0) // HEAD_SIZE
    c = lax.broadcasted_iota(jnp.int32, (LANES, LANES), 1) // HEAD_SIZE
    return (r == c).astype(BF16)


def _head_sum(x, ones):
    outs = []
    for c in range(HEAD_PAIRS):
        xc = x[:, c * LANES:(c + 1) * LANES]
        hi = xc.astype(BF16)
        lo = (xc - hi.astype(F32)).astype(BF16)
        outs.append(_dot(hi, ones) + _dot(lo, ones))
    return jnp.concatenate(outs, axis=-1)


def _head_dot(x):
    head0 = lax.broadcasted_iota(jnp.int32, x.shape, 1) < HEAD_SIZE
    s0 = jnp.sum(jnp.where(head0, x, 0.0), axis=1, keepdims=True)
    s1 = jnp.sum(jnp.where(head0, 0.0, x), axis=1, keepdims=True)
    return jnp.where(head0, s0, s1)


def _ada_kernel(c_ref, w_ref, b_ref, o_ref):
    c = c_ref[...]
    s = (c * jax.nn.sigmoid(c)).astype(BF16)
    o_ref[...] = _dot(s, w_ref[...].astype(BF16)) + b_ref[...]


def _ada(c_all, ada_w, ada_b):
    n = c_all.shape[0]
    nd = N_MOD * D_MODEL
    return pl.pallas_call(
        _ada_kernel,
        grid=(DEPTH, nd // ADA_TILE),
        in_specs=[
            pl.BlockSpec((n, D_MODEL), lambda l, j: (0, 0)),
            pl.BlockSpec((None, D_MODEL, ADA_TILE), lambda l, j: (l, 0, j)),
            pl.BlockSpec((None, 1, ADA_TILE), lambda l, j: (l, 0, j)),
        ],
        out_specs=pl.BlockSpec((None, n, ADA_TILE), lambda l, j: (l, 0, j)),
        out_shape=jax.ShapeDtypeStruct((DEPTH, n, nd), F32),
        compiler_params=_params("parallel", "parallel"),
        name="ada_mod",
    )(c_all, ada_w, ada_b.reshape(DEPTH, 1, nd))


def _mod_spec(bsz, layer, m):
    return pl.BlockSpec((None, bsz, D_MODEL), lambda *_: (layer, 0, m))


def _time_tile(t_len, bsz, rows=ROW_TILE):
    return max(1, min(t_len, rows // bsz))


def _ffn_kernel(x_ref, sh_ref, sc_ref, g_ref, win_ref, wout_ref, o_ref, *, batch_major_in):
    x = x_ref[...]
    if batch_major_in:
        x = jnp.swapaxes(x, 0, 1)
    tt, bsz, d = x.shape
    h = _rms_mod(x, sh_ref[...], sc_ref[...]).reshape(tt * bsz, d).astype(BF16)
    acc = jnp.zeros((tt * bsz, d), F32)
    for j in range(D_FF // FFN_CHUNK):
        lo = j * FFN_CHUNK
        gate = _dot(h, win_ref[:, lo:lo + FFN_CHUNK].astype(BF16))
        up = _dot(h, win_ref[:, D_FF + lo:D_FF + lo + FFN_CHUNK].astype(BF16))
        act = (gate * jax.nn.sigmoid(gate) * up).astype(BF16)
        acc = acc + _dot(act, wout_ref[lo:lo + FFN_CHUNK, :].astype(BF16))
    o_ref[...] = x + (FFN_RES * (1.0 + g_ref[...]))[None] * acc.reshape(tt, bsz, d)


def _ffn(x, mods, layer, sub, w_in, w_out, batch_major_in=False):
    if batch_major_in:
        bsz, t_len, d = x.shape
    else:
        t_len, bsz, d = x.shape
    tt = _time_tile(t_len, bsz)
    m0 = 0 if sub == 0 else 6
    xspec = pl.BlockSpec((tt, bsz, d), lambda i: (i, 0, 0))
    return pl.pallas_call(
        functools.partial(_ffn_kernel, batch_major_in=batch_major_in),
        grid=(t_len // tt,),
        in_specs=[
            pl.BlockSpec((bsz, tt, d), lambda i: (0, i, 0)) if batch_major_in else xspec,
            _mod_spec(bsz, layer, m0), _mod_spec(bsz, layer, m0 + 1), _mod_spec(bsz, layer, m0 + 2),
            _resident((None, None, d, 2 * D_FF), lambda i: (layer, sub, 0, 0)),
            _resident((None, None, D_FF, d), lambda i: (layer, sub, 0, 0)),
        ],
        out_specs=xspec,
        out_shape=jax.ShapeDtypeStruct((t_len, bsz, d), F32),
        compiler_params=_params("parallel"),
        name="ffn",
    )(x, mods, mods, mods, w_in, w_out)


def _lru_kernel(x_ref, sh_ref, sc_ref, g_ref, h0_ref, conv0_ref, win_ref, cw_ref, cb_ref, gw_ref,
                gb_ref, lam_ref, wout_ref, o_ref, hlast_ref, buf_ref, a_scr, u_scr):
    @pl.when(pl.program_id(0) == 0)
    def _():
        hlast_ref[...] = h0_ref[...]
        buf_ref[...] = conv0_ref[...]

    x = x_ref[...]
    tt, bsz, d = x.shape
    tm = tt * bsz
    h = _rms_mod(x, sh_ref[...], sc_ref[...]).reshape(tm, d).astype(BF16)
    proj = _dot(h, win_ref[...])
    gate_branch = proj[:, :d]
    xp = jnp.concatenate([buf_ref[...], proj[:, d:].reshape(tt, bsz, d)], axis=0)
    cw = cw_ref[...]
    xc = cb_ref[...][None]
    for j in range(CONV_W):
        xc = xc + xp[j:j + tt] * cw[j:j + 1][None]
    buf_ref[...] = xp[tt:tt + CONV_W - 1]
    xc = xc.reshape(tm, d)

    lam = lam_ref[...]
    log_sig_lam = -_softplus(-lam)
    for n in range(LRU_BLOCKS):
        sl = slice(n * LRU_BLOCK_W, (n + 1) * LRU_BLOCK_W)
        xb = xc[:, sl]
        gates = _dot(xb.astype(BF16), gw_ref[n]) + gb_ref[n:n + 1, :]
        r = jax.nn.sigmoid(gates[:, :LRU_BLOCK_W])
        ig = jax.nn.sigmoid(gates[:, LRU_BLOCK_W:])
        log_a = RG_C * r * log_sig_lam[:, sl]
        a_scr[:, sl] = jnp.exp(log_a)
        th = jnp.tanh(log_a)
        u_scr[:, sl] = jnp.sqrt(-2.0 * th / (1.0 - th)) * (ig * xb)

    cwid = min(d, SCAN_CARRY_VREGS * SUBLANES * LANES // bsz)
    for c in range(d // cwid):
        ls = slice(c * cwid, (c + 1) * cwid)

        def step(t, hc, ls=ls):
            rows = pl.ds(pl.multiple_of(t * bsz, bsz), bsz)
            hc = a_scr[rows, ls] * hc + u_scr[rows, ls]
            u_scr[rows, ls] = hc
            return hc

        hlast_ref[:, ls] = lax.fori_loop(0, tt, step, hlast_ref[:, ls])

    gelu = 0.5 * gate_branch * (1.0 + jnp.tanh(
        0.7978845608028654 * (gate_branch + 0.044715 * (gate_branch * gate_branch * gate_branch))))
    y = _dot((u_scr[...] * gelu).astype(BF16), wout_ref[...])
    o_ref[...] = x + (1.0 + g_ref[...])[None] * y.reshape(tt, bsz, d)


def _lru(x, mods, layer, h0, conv0, w_in, conv_w, conv_b, gate_w, gate_b, lam, w_out):
    t_len, bsz, d = x.shape
    tt = _time_tile(t_len, bsz)
    xspec = pl.BlockSpec((tt, bsz, d), lambda i: (i, 0, 0))

    def full(a):
        return _resident(a.shape, lambda i, n=a.ndim: (0,) * n)

    args = (h0, conv0, w_in, conv_w, conv_b, gate_w, gate_b, lam, w_out)
    return pl.pallas_call(
        _lru_kernel,
        grid=(t_len // tt,),
        in_specs=[xspec, _mod_spec(bsz, layer, 3), _mod_spec(bsz, layer, 4), _mod_spec(bsz, layer, 5)]
        + [full(a) for a in args],
        out_specs=[xspec,
                   pl.BlockSpec((bsz, d), lambda i: (0, 0)),
                   pl.BlockSpec((CONV_W - 1, bsz, d), lambda i: (0, 0, 0))],
        out_shape=[jax.ShapeDtypeStruct(x.shape, F32),
                   jax.ShapeDtypeStruct((bsz, d), F32),
                   jax.ShapeDtypeStruct((CONV_W - 1, bsz, d), F32)],
        scratch_shapes=[pltpu.VMEM((tt * bsz, d), F32), pltpu.VMEM((tt * bsz, d), F32)],
        compiler_params=_params("arbitrary"),
        name="rglru",
    )(x, mods, mods, mods, *args)


def _rwkv_in_kernel(*refs, has_vres):
    (x_ref, sh_ref, sc_ref, shift0_ref, mu_ref, wrkv_ref, w0_ref, w1_ref, w2_ref, a0_ref, a1_ref,
     a2_ref, g1_ref, g2_ref, kk_ref, ka_ref) = refs[:16]
    refs = refs[16:]
    if has_vres:
        v0_ref, v1_ref, v2_ref, vfirst_ref = refs[:4]
        refs = refs[4:]
    r_out, w_out, k_out, v_out, na_out, b_out, g_out, shift_ref = refs

    @pl.when(pl.program_id(0) == 0)
    def _():
        shift_ref[...] = shift0_ref[...]

    x = x_ref[...]
    tt, bsz, d = x.shape
    tm = tt * bsz
    xf = _rms_mod(x, sh_ref[...], sc_ref[...])
    prev = jnp.concatenate([shift_ref[...][None], xf[:tt - 1]], axis=0)
    shift_ref[...] = xf[tt - 1]
    xf_b = xf.reshape(tm, d).astype(BF16)
    dx_b = (prev - xf).reshape(tm, d).astype(BF16)

    def mix(j):
        return xf_b + dx_b * mu_ref[j:j + 1, :].astype(BF16)

    r = _dot(mix(0), wrkv_ref[0])
    k = _dot(mix(1), wrkv_ref[1])
    xm_v = mix(2)
    v = _dot(xm_v, wrkv_ref[2])
    z = w0_ref[...] + _dot(jnp.tanh(_dot(mix(3), w1_ref[...])).astype(BF16), w2_ref[...])
    decay = jnp.exp(-DECAY_SCALE * jax.nn.sigmoid(z))
    if has_vres:
        mix_v = jax.nn.sigmoid(v0_ref[...] + _dot(_dot(xm_v, v1_ref[...]).astype(BF16), v2_ref[...]))
        v = v + (vfirst_ref[...].reshape(tm, d) - v) * mix_v
    a = jax.nn.sigmoid(a0_ref[...] + _dot(_dot(mix(4), a1_ref[...]).astype(BF16), a2_ref[...]))
    g = _dot(jax.nn.sigmoid(_dot(mix(5), g1_ref[...])).astype(BF16), g2_ref[...])

    shp = (tt, bsz, d)
    r_out[...] = r.reshape(shp)
    w_out[...] = decay.reshape(shp)
    k_out[...] = (k * (1.0 + (a - 1.0) * ka_ref[...])).reshape(shp)
    v_out[...] = v.reshape(shp)
    kk = k * kk_ref[...]
    norm = jnp.concatenate([jnp.sqrt(_head_dot(kk[:, c * LANES:(c + 1) * LANES] ** 2))
                            for c in range(HEAD_PAIRS)], axis=1)
    kk = kk / jnp.maximum(norm, 1e-12)
    na_out[...] = (-kk).reshape(shp)
    b_out[...] = (kk * a).reshape(shp)
    g_out[...] = g.reshape(shp)


def _rwkv_in(x, mods, layer, shift0, weights, vres):
    t_len, bsz, d = x.shape
    tt = _time_tile(t_len, bsz, RWKV_IN_ROW_TILE)
    xspec = pl.BlockSpec((tt, bsz, d), lambda i: (i, 0, 0))

    def full(a):
        return _resident(a.shape, lambda i, n=a.ndim: (0,) * n)

    args = [shift0] + list(weights)
    in_specs = [xspec, _mod_spec(bsz, layer, 3), _mod_spec(bsz, layer, 4)] + [full(a) for a in args]
    if vres is not None:
        v0, v1, v2, v_first = vres
        args += [v0, v1, v2, v_first]
        in_specs += [full(v0), full(v1), full(v2), xspec]
    act = jax.ShapeDtypeStruct(x.shape, F32)
    return pl.pallas_call(
        functools.partial(_rwkv_in_kernel, has_vres=vres is not None),
        grid=(t_len // tt,),
        in_specs=in_specs,
        out_specs=[xspec] * 7 + [pl.BlockSpec((bsz, d), lambda i: (0, 0))],
        out_shape=[act] * 7 + [jax.ShapeDtypeStruct((bsz, d), F32)],
        compiler_params=_params("arbitrary"),
        name="rwkv_in",
    )(x, mods, mods, *args)


def _wkv_kernel(*refs, has_state):
    r_ref, w_ref, k_ref, v_ref, a_ref, b_ref = refs[:6]
    if has_state:
        s0_ref, y_ref, s_out_ref, s_scr = refs[6:]
    else:
        y_ref, s_out_ref, s_scr = refs[6:]
    tb, bb = r_ref.shape[0], r_ref.shape[1]

    @pl.when(pl.program_id(1) == 0)
    def _():
        if has_state:
            def pack(b, carry):
                for p in range(HEAD_PAIRS):
                    s_scr[b, p] = jnp.concatenate([s0_ref[b, 2 * p], s0_ref[b, 2 * p + 1]], axis=1)
                return carry
            lax.fori_loop(0, bb, pack, 0)
        else:
            s_scr[...] = jnp.zeros_like(s_scr)

    s_ref = s_scr
    head0 = lax.broadcasted_iota(jnp.int32, (SUBLANES, LANES), 1) < HEAD_SIZE
    rows_s = lax.broadcasted_iota(jnp.int32, (WKV_STATE_COLS, LANES), 0)
    rows_y = lax.broadcasted_iota(jnp.int32, (WKV_COLS, 2 * LANES), 0)
    seq_half = lax.broadcasted_iota(jnp.int32, (WKV_COLS, 2 * LANES), 1) // LANES
    zero8 = jnp.zeros((SUBLANES, LANES), F32)
    zero_rows = jnp.zeros((WKV_STATE_COLS, HEAD_SIZE), F32)
    contract_lanes = (((1,), (1,)), ((), ()))

    def split(x):
        return [jnp.where(head0, x, 0.0), jnp.where(head0, 0.0, x)]

    head_dot = _head_dot

    def halves(x):
        return [x[:, :HEAD_SIZE], x[:, HEAD_SIZE:]]

    def load(ref, t, b0, p):
        return ref[t, pl.ds(b0, SUBLANES), pl.ds(p * LANES, LANES)]

    def stage_a(c, b0, p):
        t1, t2 = 2 * c, 2 * c + 1
        w1 = load(w_ref, t1, b0, p)
        vectors = [load(a_ref, t1, b0, p), load(a_ref, t2, b0, p) * w1, w1 * load(r_ref, t1, b0, p),
                   w1 * load(w_ref, t2, b0, p) * load(r_ref, t2, b0, p)]
        lmat = jnp.concatenate([h for x in vectors for h in split(x)], axis=0).astype(BF16)
        none = jnp.zeros_like(lmat)
        proj = None
        for i in range(SUBLANES):
            l_i = jnp.where(rows_s % SUBLANES == i, lmat, none)
            d = lax.dot_general(s_ref[b0 + i, p].astype(BF16), l_i, contract_lanes,
                                preferred_element_type=F32)
            proj = d if proj is None else proj + d
        return proj

    def stage_b(c, b0, p, proj):
        t1, t2 = 2 * c, 2 * c + 1
        r1, w1, k1, v1, b1 = (load(ref, t1, b0, p) for ref in (r_ref, w_ref, k_ref, v_ref, b_ref))
        r2, w2, k2, v2, a2, b2 = (load(ref, t2, b0, p) for ref in (r_ref, w_ref, k_ref, v_ref, a_ref, b_ref))
        w12 = w1 * w2
        b1a2 = head_dot(b1 * a2)
        k1a2 = head_dot(k1 * a2)
        ymat = jnp.concatenate(split(b1 * w2 + b1a2 * b2) + split(b2) + [zero8] * 4
                               + split(k1 * w2 + k1a2 * b2) + split(k2), axis=0).astype(BF16)
        ymat = jnp.concatenate([ymat, ymat], axis=1)
        none = jnp.zeros_like(ymat)
        v_rows = jnp.concatenate([zero_rows] + halves(v1) + halves(v2), axis=0)
        pb = jnp.concatenate([proj, v_rows.T[:, WKV_STATE_COLS:]], axis=1).astype(BF16)
        for i in range(0, SUBLANES, 2):
            upd = _dot(pb, jnp.where(rows_y % SUBLANES == i + seq_half, ymat, none))
            s_ref[b0 + i, p] = s_ref[b0 + i, p] * w12[i:i + 1] + upd[:, :LANES]
            s_ref[b0 + i + 1, p] = s_ref[b0 + i + 1, p] * w12[i + 1:i + 2] + upd[:, LANES:]
        pt = proj.T

        def rows_of(j):
            lo = 2 * j * SUBLANES
            return jnp.concatenate([pt[lo:lo + SUBLANES], pt[lo + SUBLANES:lo + 2 * SUBLANES]], axis=1)

        u1, s0a2, s0r1, s0r2 = rows_of(0), rows_of(1), rows_of(2), rows_of(3)
        b2r2 = head_dot(b2 * r2)
        w2r2 = w2 * r2
        y_ref[t1, pl.ds(b0, SUBLANES), pl.ds(p * LANES, LANES)] = (
            s0r1 + u1 * head_dot(b1 * r1) + v1 * head_dot(k1 * r1))
        y_ref[t2, pl.ds(b0, SUBLANES), pl.ds(p * LANES, LANES)] = (
            s0r2 + u1 * (head_dot(b1 * w2r2) + b1a2 * b2r2) + v1 * (head_dot(k1 * w2r2) + k1a2 * b2r2)
            + s0a2 * b2r2 + v2 * head_dot(k2 * r2))

    tiles = [(g * SUBLANES, p) for g in range(bb // SUBLANES) for p in range(HEAD_PAIRS)]
    n_chunks = tb // 2
    unroll = max(u for u in range(1, WKV_CHUNK_UNROLL + 1) if n_chunks % u == 0)

    def t_body(cu, carry):
        work = [(cu * unroll + u, b0, p) for u in range(unroll) for (b0, p) in tiles]
        pending = {}
        for step in range(len(work) + WKV_LAG):
            if step < len(work):
                pending[step] = stage_a(*work[step])
            if step >= WKV_LAG:
                stage_b(*work[step - WKV_LAG], pending.pop(step - WKV_LAG))
        return carry

    lax.fori_loop(0, n_chunks // unroll, t_body, 0)

    @pl.when(pl.program_id(1) == pl.num_programs(1) - 1)
    def _():
        def unpack(b, carry):
            for p in range(HEAD_PAIRS):
                s = s_scr[b, p]
                s_out_ref[b, 2 * p] = s[:, :HEAD_SIZE]
                s_out_ref[b, 2 * p + 1] = s[:, HEAD_SIZE:]
            return carry
        lax.fori_loop(0, bb, unpack, 0)


def _wkv(r, w, k, v, a, b, s0):
    t_len, bsz, d = r.shape
    bb = min(bsz, WKV_BATCH_TILE)
    tb = min(t_len, WKV_TIME_TILE)
    assert tb % 2 == 0 and t_len % tb == 0 and bsz % bb == 0
    xspec = pl.BlockSpec((tb, bb, d), lambda i, j: (j, i, 0))
    sspec = pl.BlockSpec((bb, RWKV_HEADS, HEAD_SIZE, HEAD_SIZE), lambda i, j: (i, 0, 0, 0))
    has_state = s0 is not None
    return pl.pallas_call(
        functools.partial(_wkv_kernel, has_state=has_state),
        grid=(bsz // bb, t_len // tb),
        in_specs=[xspec] * 6 + ([sspec] if has_state else []),
        out_specs=[xspec, sspec],
        out_shape=[jax.ShapeDtypeStruct(r.shape, F32),
                   jax.ShapeDtypeStruct((bsz, RWKV_HEADS, HEAD_SIZE, HEAD_SIZE), F32)],
        scratch_shapes=[pltpu.VMEM((bb, HEAD_PAIRS, HEAD_SIZE, LANES), F32)],
        compiler_params=_params("parallel", "arbitrary"),
        name="wkv",
    )(r, w, k, v, a, b, *([s0] if has_state else []))


def _rwkv_out_kernel(y_ref, r_ref, k_ref, v_ref, g_ref, x_ref, gate_ref, lnw_ref, lnb_ref, rk_ref,
                     wo_ref, o_ref):
    x = x_ref[...]
    tt, bsz, d = x.shape
    tm = tt * bsz
    ones = _head_ones()
    y = y_ref[...].reshape(tm, d)
    yc = y - _head_sum(y, ones) * (1.0 / HEAD_SIZE)
    var = _head_sum(yc * yc, ones) * (1.0 / HEAD_SIZE)
    yn = yc * lax.rsqrt(var + GN_EPS) * lnw_ref[...] + lnb_ref[...]
    rk = r_ref[...].reshape(tm, d) * k_ref[...].reshape(tm, d) * rk_ref[...]
    bonus = _head_sum(rk, ones) * v_ref[...].reshape(tm, d)
    out = _dot(((yn + bonus) * g_ref[...].reshape(tm, d)).astype(BF16), wo_ref[...])
    o_ref[...] = x + (1.0 + gate_ref[...])[None] * out.reshape(tt, bsz, d)


def _rwkv_out(y, r, k, v, g, x, mods, layer, ln_w, ln_b, r_k, w_o):
    t_len, bsz, d = x.shape
    tt = _time_tile(t_len, bsz)
    xspec = pl.BlockSpec((tt, bsz, d), lambda i: (i, 0, 0))

    def full(a):
        return _resident(a.shape, lambda i, n=a.ndim: (0,) * n)

    return pl.pallas_call(
        _rwkv_out_kernel,
        grid=(t_len // tt,),
        in_specs=[xspec] * 6 + [_mod_spec(bsz, layer, 5)] + [full(a) for a in (ln_w, ln_b, r_k, w_o)],
        out_specs=xspec,
        out_shape=jax.ShapeDtypeStruct(x.shape, F32),
        compiler_params=_params("parallel"),
        name="rwkv_out",
    )(y, r, k, v, g, x, mods, ln_w, ln_b, r_k, w_o)


def _final_kernel(x_ref, gain_ref, o_ref, *, batch_major_out):
    x = x_ref[...]
    ms = jnp.mean(x * x, axis=-1, keepdims=True)
    y = (x * lax.rsqrt(ms + NORM_EPS)) * gain_ref[...][None]
    o_ref[...] = jnp.swapaxes(y, 0, 1) if batch_major_out else y


def _batch_major_tiles(t_len, bsz):
    return _time_tile(t_len, bsz) % SUBLANES == 0


def _final(x, gain, batch_major_out):
    t_len, bsz, d = x.shape
    tt = _time_tile(t_len, bsz)
    xspec = pl.BlockSpec((tt, bsz, d), lambda i: (i, 0, 0))
    return pl.pallas_call(
        functools.partial(_final_kernel, batch_major_out=batch_major_out),
        grid=(t_len // tt,),
        in_specs=[xspec, pl.BlockSpec((1, d), lambda i: (0, 0))],
        out_specs=pl.BlockSpec((bsz, tt, d), lambda i: (0, i, 0)) if batch_major_out else xspec,
        out_shape=jax.ShapeDtypeStruct((bsz, t_len, d) if batch_major_out else x.shape, F32),
        compiler_params=_params("parallel"),
        name="final_norm",
    )(x, gain)


def _pad_cols(w):
    n = w.shape[-1]
    return jnp.pad(w, [(0, 0)] * (w.ndim - 1) + [(0, -n % LORA_PAD)])


def _pad_rows(w):
    n = w.shape[-2]
    return jnp.pad(w, [(0, 0)] * (w.ndim - 2) + [(0, -n % LORA_PAD), (0, 0)])


def _trunk(x, mods, lru_h0, lru_conv0, rwkv_shift0, rwkv_wkv0, W):
    new_h, new_conv, new_shift, new_wkv = [], [], [], []
    v_first = None
    fold = _batch_major_tiles(x.shape[1], x.shape[0])
    if not fold:
        x = jnp.transpose(x, (1, 0, 2))
    for layer in range(DEPTH):
        j = layer // 2
        x = _ffn(x, mods, layer, 0, W['ffn_w_in'], W['ffn_w_out'], batch_major_in=fold and layer == 0)
        if layer % 2 == 0:
            x, h_last, buf = _lru(x, mods, layer, lru_h0[j], lru_conv0[j], W['lru_w_in'][j],
                                  W['lru_conv_w'][j], W['lru_conv_b'][j], W['lru_gate_w'][j],
                                  W['lru_gate_b'][j], W['lru_lambda'][j], W['lru_w_out'][j])
            new_h.append(h_last)
            new_conv.append(buf)
        else:
            weights = [W[n][j] for n in ('rwkv_mu', 'rwkv_w_rkv', 'rwkv_w0', 'rwkv_w1', 'rwkv_w2',
                                         'rwkv_a0', 'rwkv_a1', 'rwkv_a2', 'rwkv_g1', 'rwkv_g2',
                                         'rwkv_k_k', 'rwkv_k_a')]
            vres = None if j == 0 else (W['rwkv_v0'][j - 1], W['rwkv_v1'][j - 1], W['rwkv_v2'][j - 1],
                                        v_first)
            r, w, k, v, na, b, g, shift_last = _rwkv_in(x, mods, layer, rwkv_shift0[j], weights, vres)
            if v_first is None:
                v_first = v
            y, s_last = _wkv(r, w, k, v, na, b, None if rwkv_wkv0 is None else rwkv_wkv0[j])
            x = _rwkv_out(y, r, k, v, g, x, mods, layer, W['rwkv_ln_w'][j], W['rwkv_ln_b'][j],
                          W['rwkv_r_k'][j], W['rwkv_w_o'][j])
            new_shift.append(shift_last)
            new_wkv.append(s_last)
        x = _ffn(x, mods, layer, 1, W['ffn_w_in'], W['ffn_w_out'])
    y = _final(x, W['final_gain'], batch_major_out=fold)
    if not fold:
        y = jnp.transpose(y, (1, 0, 2))
    return y, jnp.stack(new_h), jnp.stack(new_conv), jnp.stack(new_shift), jnp.stack(new_wkv)


def _prep_weights(P):
    row = lambda a: a.reshape(a.shape[0], 1, -1)
    W = {
        'ffn_w_in': P['ffn_w_in'],
        'ffn_w_out': P['ffn_w_out'],
        'lru_w_in': P['lru_w_in'].astype(BF16),
        'lru_conv_w': P['lru_conv_w'],
        'lru_conv_b': row(P['lru_conv_b']),
        'lru_gate_w': P['lru_gate_w'].astype(BF16),
        'lru_gate_b': P['lru_gate_b'],
        'lru_lambda': row(P['lru_lambda']),
        'lru_w_out': P['lru_w_out'].astype(BF16),
        'rwkv_mu': P['rwkv_mu'],
        'rwkv_w_rkv': P['rwkv_w_rkv'].astype(BF16),
        'rwkv_w_o': P['rwkv_w_o'].astype(BF16),
        'rwkv_r_k': row(P['rwkv_r_k']),
        'final_gain': P['final_gain'].reshape(1, -1),
    }
    for n in ('rwkv_w0', 'rwkv_a0', 'rwkv_v0', 'rwkv_k_k', 'rwkv_k_a', 'rwkv_ln_w', 'rwkv_ln_b'):
        W[n] = row(P[n])
    for n in ('rwkv_w1', 'rwkv_a1', 'rwkv_v1', 'rwkv_g1'):
        W[n] = _pad_cols(P[n]).astype(BF16)
    for n in ('rwkv_w2', 'rwkv_a2', 'rwkv_v2', 'rwkv_g2'):
        W[n] = _pad_rows(P[n]).astype(BF16)
    return W


def _run_stream(x, mods, lru_h, lru_conv, rwkv_shift, rwkv_wkv, W):
    y, h, conv, shift, wkv = _trunk(x, mods, lru_h, jnp.transpose(lru_conv, (0, 2, 1, 3)), rwkv_shift,
                                    rwkv_wkv, W)
    return y, h, jnp.transpose(conv, (0, 2, 1, 3)), shift, wkv


def kernel(x_prompt, x_sample, state_lru_h, state_lru_conv, state_rwkv_shift, state_rwkv_wkv, c_prompt, c_sample, ada_w, ada_b, ffn_w_in, ffn_w_out, lru_w_in, lru_conv_w, lru_conv_b, lru_gate_w, lru_gate_b, lru_lambda, lru_w_out, rwkv_mu, rwkv_w_rkv, rwkv_w_o, rwkv_w0, rwkv_w1, rwkv_w2, rwkv_a0, rwkv_a1, rwkv_a2, rwkv_v0, rwkv_v1, rwkv_v2, rwkv_g1, rwkv_g2, rwkv_k_k, rwkv_k_a, rwkv_r_k, rwkv_ln_w, rwkv_ln_b, final_gain):
    P = dict(ffn_w_in=ffn_w_in, ffn_w_out=ffn_w_out,
             lru_w_in=lru_w_in, lru_conv_w=lru_conv_w, lru_conv_b=lru_conv_b, lru_gate_w=lru_gate_w,
             lru_gate_b=lru_gate_b, lru_lambda=lru_lambda, lru_w_out=lru_w_out,
             rwkv_mu=rwkv_mu, rwkv_w_rkv=rwkv_w_rkv, rwkv_w_o=rwkv_w_o, rwkv_w0=rwkv_w0, rwkv_w1=rwkv_w1,
             rwkv_w2=rwkv_w2, rwkv_a0=rwkv_a0, rwkv_a1=rwkv_a1, rwkv_a2=rwkv_a2, rwkv_v0=rwkv_v0,
             rwkv_v1=rwkv_v1, rwkv_v2=rwkv_v2, rwkv_g1=rwkv_g1, rwkv_g2=rwkv_g2, rwkv_k_k=rwkv_k_k,
             rwkv_k_a=rwkv_k_a, rwkv_r_k=rwkv_r_k, rwkv_ln_w=rwkv_ln_w, rwkv_ln_b=rwkv_ln_b,
             final_gain=final_gain)
    W = _prep_weights(P)
    bp, bs = x_prompt.shape[0], x_sample.shape[0]
    mods = _ada(jnp.concatenate([c_prompt, c_sample], axis=0), ada_w, ada_b)
    n_lru, n_rwkv = state_lru_h.shape[0], state_rwkv_shift.shape[0]
    zeros = lambda *s: jnp.zeros(s, F32)
    out_p = _run_stream(x_prompt, mods[:, :bp], zeros(n_lru, bp, D_RNN), zeros(n_lru, bp, CONV_W - 1, D_RNN),
                        zeros(n_rwkv, bp, D_MODEL), None, W)
    out_s = _run_stream(x_sample, mods[:, bp:], state_lru_h, state_lru_conv, state_rwkv_shift,
                        state_rwkv_wkv, W)
    return (out_p[0], out_s[0]) + out_p[1:] + out_s[1:]
```

```python
import functools

import jax
import jax.numpy as jnp
from jax import lax
from jax.experimental import pallas as pl
from jax.experimental.pallas import tpu as pltpu

D_MODEL = 1024
DEPTH = 4
D_RNN = D_MODEL
LRU_BLOCKS = 4
LRU_BLOCK_W = D_RNN // LRU_BLOCKS
CONV_W = 4
RG_C = 8.0
HEAD_SIZE = 64
RWKV_HEADS = D_MODEL // HEAD_SIZE
D_FF = 2816
FFN_RES = 0.5
N_MOD = 9
NORM_EPS = 1e-6
GN_EPS = HEAD_SIZE * 1e-5
DECAY_SCALE = 0.6065306597126334

LANES = 128
SUBLANES = 8
HEAD_PAIRS = D_MODEL // LANES
LORA_PAD = 128
VMEM_LIMIT_BYTES = 56 * 1024 * 1024
ROW_TILE = 512
RWKV_IN_ROW_TILE = 256
FFN_CHUNK = 256
ADA_TILE = 1152
SCAN_CARRY_VREGS = 16
WKV_STATE_COLS = 8 * SUBLANES
WKV_COLS = 12 * SUBLANES
WKV_BATCH_TILE = 8
WKV_TIME_TILE = 32
WKV_CHUNK_UNROLL = 4
WKV_LAG = 3

F32 = jnp.float32
BF16 = jnp.bfloat16


def _dot(a, b):
    return jnp.dot(a, b, preferred_element_type=F32)


def _params(*sem):
    return pltpu.CompilerParams(dimension_semantics=sem, vmem_limit_bytes=VMEM_LIMIT_BYTES)


def _resident(block_shape, index_map):
    return pl.BlockSpec(block_shape, index_map, pipeline_mode=pl.Buffered(1))


def _rms_mod(x, shift, scale):
    ms = jnp.mean(x * x, axis=-1, keepdims=True)
    return (x * lax.rsqrt(ms + NORM_EPS)) * (1.0 + scale)[None] + shift[None]


def _softplus(x):
    return jnp.maximum(x, 0.0) + jnp.log1p(jnp.exp(-jnp.abs(x)))


def _head_ones():
    r = lax.broadcasted_iota(jnp.int32, (LANES, LANES), 0) // HEAD_SIZE
    c = lax.broadcasted_iota(jnp.int32, (LANES, LANES), 1) // HEAD_SIZE
    return (r == c).astype(BF16)


def _head_sum(x, ones):
    outs = []
    for c in range(HEAD_PAIRS):
        xc = x[:, c * LANES:(c + 1) * LANES]
        hi = xc.astype(BF16)
        lo = (xc - hi.astype(F32)).astype(BF16)
        outs.append(_dot(hi, ones) + _dot(lo, ones))
    return jnp.concatenate(outs, axis=-1)


def _head_dot(x):
    head0 = lax.broadcasted_iota(jnp.int32, x.shape, 1) < HEAD_SIZE
    s0 = jnp.sum(jnp.where(head0, x, 0.0), axis=1, keepdims=True)
    s1 = jnp.sum(jnp.where(head0, 0.0, x), axis=1, keepdims=True)
    return jnp.where(head0, s0, s1)


def _ada_kernel(c_ref, w_ref, b_ref, o_ref):
    c = c_ref[...]
    s = (c * jax.nn.sigmoid(c)).astype(BF16)
    o_ref[...] = _dot(s, w_ref[...].astype(BF16)) + b_ref[...]


def _ada(c_all, ada_w, ada_b):
    n = c_all.shape[0]
    nd = N_MOD * D_MODEL
    return pl.pallas_call(
        _ada_kernel,
        grid=(DEPTH, nd // ADA_TILE),
        in_specs=[
            pl.BlockSpec((n, D_MODEL), lambda l, j: (0, 0)),
            pl.BlockSpec((None, D_MODEL, ADA_TILE), lambda l, j: (l, 0, j)),
            pl.BlockSpec((None, 1, ADA_TILE), lambda l, j: (l, 0, j)),
        ],
        out_specs=pl.BlockSpec((None, n, ADA_TILE), lambda l, j: (l, 0, j)),
        out_shape=jax.ShapeDtypeStruct((DEPTH, n, nd), F32),
        compiler_params=_params("parallel", "parallel"),
        name="ada_mod",
    )(c_all, ada_w, ada_b.reshape(DEPTH, 1, nd))


def _mod_spec(bsz, layer, m):
    return pl.BlockSpec((None, bsz, D_MODEL), lambda *_: (layer, 0, m))


def _time_tile(t_len, bsz, rows=ROW_TILE):
    return max(1, min(t_len, rows // bsz))


def _ffn_kernel(x_ref, sh_ref, sc_ref, g_ref, win_ref, wout_ref, o_ref, *, batch_major_in):
    x = x_ref[...]
    if batch_major_in:
        x = jnp.swapaxes(x, 0, 1)
    tt, bsz, d = x.shape
    h = _rms_mod(x, sh_ref[...], sc_ref[...]).reshape(tt * bsz, d).astype(BF16)
    acc = jnp.zeros((tt * bsz, d), F32)
    for j in range(D_FF // FFN_CHUNK):
        lo = j * FFN_CHUNK
        gate = _dot(h, win_ref[:, lo:lo + FFN_CHUNK].astype(BF16))
        up = _dot(h, win_ref[:, D_FF + lo:D_FF + lo + FFN_CHUNK].astype(BF16))
        act = (gate * jax.nn.sigmoid(gate) * up).astype(BF16)
        acc = acc + _dot(act, wout_ref[lo:lo + FFN_CHUNK, :].astype(BF16))
    o_ref[...] = x + (FFN_RES * (1.0 + g_ref[...]))[None] * acc.reshape(tt, bsz, d)


def _ffn(x, mods, layer, sub, w_in, w_out, batch_major_in=False):
    if batch_major_in:
        bsz, t_len, d = x.shape
    else:
        t_len, bsz, d = x.shape
    tt = _time_tile(t_len, bsz)
    m0 = 0 if sub == 0 else 6
    xspec = pl.BlockSpec((tt, bsz, d), lambda i: (i, 0, 0))
    return pl.pallas_call(
        functools.partial(_ffn_kernel, batch_major_in=batch_major_in),
        grid=(t_len // tt,),
        in_specs=[
            pl.BlockSpec((bsz, tt, d), lambda i: (0, i, 0)) if batch_major_in else xspec,
            _mod_spec(bsz, layer, m0), _mod_spec(bsz, layer, m0 + 1), _mod_spec(bsz, layer, m0 + 2),
            _resident((None, None, d, 2 * D_FF), lambda i: (layer, sub, 0, 0)),
            _resident((None, None, D_FF, d), lambda i: (layer, sub, 0, 0)),
        ],
        out_specs=xspec,
        out_shape=jax.ShapeDtypeStruct((t_len, bsz, d), F32),
        compiler_params=_params("parallel"),
        name="ffn",
    )(x, mods, mods, mods, w_in, w_out)


def _lru_kernel(x_ref, sh_ref, sc_ref, g_ref, h0_ref, conv0_ref, win_ref, cw_ref, cb_ref, gw_ref,
                gb_ref, lam_ref, wout_ref, o_ref, hlast_ref, buf_ref, a_scr, u_scr):
    @pl.when(pl.program_id(0) == 0)
    def _():
        hlast_ref[...] = h0_ref[...]
        buf_ref[...] = conv0_ref[...]

    x = x_ref[...]
    tt, bsz, d = x.shape
    tm = tt * bsz
    h = _rms_mod(x, sh_ref[...], sc_ref[...]).reshape(tm, d).astype(BF16)
    proj = _dot(h, win_ref[...])
    gate_branch = proj[:, :d]
    xp = jnp.concatenate([buf_ref[...], proj[:, d:].reshape(tt, bsz, d)], axis=0)
    cw = cw_ref[...]
    xc = cb_ref[...][None]
    for j in range(CONV_W):
        xc = xc + xp[j:j + tt] * cw[j:j + 1][None]
    buf_ref[...] = xp[tt:tt + CONV_W - 1]
    xc = xc.reshape(tm, d)

    lam = lam_ref[...]
    log_sig_lam = -_softplus(-lam)
    for n in range(LRU_BLOCKS):
        sl = slice(n * LRU_BLOCK_W, (n + 1) * LRU_BLOCK_W)
        xb = xc[:, sl]
        gates = _dot(xb.astype(BF16), gw_ref[n]) + gb_ref[n:n + 1, :]
        r = jax.nn.sigmoid(gates[:, :LRU_BLOCK_W])
        ig = jax.nn.sigmoid(gates[:, LRU_BLOCK_W:])
        log_a = RG_C * r * log_sig_lam[:, sl]
        a_scr[:, sl] = jnp.exp(log_a)
        th = jnp.tanh(log_a)
        u_scr[:, sl] = jnp.sqrt(-2.0 * th / (1.0 - th)) * (ig * xb)

    cwid = min(d, SCAN_CARRY_VREGS * SUBLANES * LANES // bsz)
    for c in range(d // cwid):
        ls = slice(c * cwid, (c + 1) * cwid)

        def step(t, hc, ls=ls):
            rows = pl.ds(pl.multiple_of(t * bsz, bsz), bsz)
            hc = a_scr[rows, ls] * hc + u_scr[rows, ls]
            u_scr[rows, ls] = hc
            return hc

        hlast_ref[:, ls] = lax.fori_loop(0, tt, step, hlast_ref[:, ls])

    gelu = 0.5 * gate_branch * (1.0 + jnp.tanh(
        0.7978845608028654 * (gate_branch + 0.044715 * (gate_branch * gate_branch * gate_branch))))
    y = _dot((u_scr[...] * gelu).astype(BF16), wout_ref[...])
    o_ref[...] = x + (1.0 + g_ref[...])[None] * y.reshape(tt, bsz, d)


def _lru(x, mods, layer, h0, conv0, w_in, conv_w, conv_b, gate_w, gate_b, lam, w_out):
    t_len, bsz, d = x.shape
    tt = _time_tile(t_len, bsz)
    xspec = pl.BlockSpec((tt, bsz, d), lambda i: (i, 0, 0))

    def full(a):
        return _resident(a.shape, lambda i, n=a.ndim: (0,) * n)

    args = (h0, conv0, w_in, conv_w, conv_b, gate_w, gate_b, lam, w_out)
    return pl.pallas_call(
        _lru_kernel,
        grid=(t_len // tt,),
        in_specs=[xspec, _mod_spec(bsz, layer, 3), _mod_spec(bsz, layer, 4), _mod_spec(bsz, layer, 5)]
        + [full(a) for a in args],
        out_specs=[xspec,
                   pl.BlockSpec((bsz, d), lambda i: (0, 0)),
                   pl.BlockSpec((CONV_W - 1, bsz, d), lambda i: (0, 0, 0))],
        out_shape=[jax.ShapeDtypeStruct(x.shape, F32),
                   jax.ShapeDtypeStruct((bsz, d), F32),
                   jax.ShapeDtypeStruct((CONV_W - 1, bsz, d), F32)],
        scratch_shapes=[pltpu.VMEM((tt * bsz, d), F32), pltpu.VMEM((tt * bsz, d), F32)],
        compiler_params=_params("arbitrary"),
        name="rglru",
    )(x, mods, mods, mods, *args)


def _rwkv_in_kernel(*refs, has_vres):
    (x_ref, sh_ref, sc_ref, shift0_ref, mu_ref, wrkv_ref, w0_ref, w1_ref, w2_ref, a0_ref, a1_ref,
     a2_ref, g1_ref, g2_ref, kk_ref, ka_ref) = refs[:16]
    refs = refs[16:]
    if has_vres:
        v0_ref, v1_ref, v2_ref, vfirst_ref = refs[:4]
        refs = refs[4:]
    r_out, w_out, k_out, v_out, na_out, b_out, g_out, shift_ref = refs

    @pl.when(pl.program_id(0) == 0)
    def _():
        shift_ref[...] = shift0_ref[...]

    x = x_ref[...]
    tt, bsz, d = x.shape
    tm = tt * bsz
    xf = _rms_mod(x, sh_ref[...], sc_ref[...])
    prev = jnp.concatenate([shift_ref[...][None], xf[:tt - 1]], axis=0)
    shift_ref[...] = xf[tt - 1]
    xf_b = xf.reshape(tm, d).astype(BF16)
    dx_b = (prev - xf).reshape(tm, d).astype(BF16)

    def mix(j):
        return xf_b + dx_b * mu_ref[j:j + 1, :].astype(BF16)

    r = _dot(mix(0), wrkv_ref[0])
    k = _dot(mix(1), wrkv_ref[1])
    xm_v = mix(2)
    v = _dot(xm_v, wrkv_ref[2])
    z = w0_ref[...] + _dot(jnp.tanh(_dot(mix(3), w1_ref[...])).astype(BF16), w2_ref[...])
    decay = jnp.exp(-DECAY_SCALE * jax.nn.sigmoid(z))
    if has_vres:
        mix_v = jax.nn.sigmoid(v0_ref[...] + _dot(_dot(xm_v, v1_ref[...]).astype(BF16), v2_ref[...]))
        v = v + (vfirst_ref[...].reshape(tm, d) - v) * mix_v
    a = jax.nn.sigmoid(a0_ref[...] + _dot(_dot(mix(4), a1_ref[...]).astype(BF16), a2_ref[...]))
    g = _dot(jax.nn.sigmoid(_dot(mix(5), g1_ref[...])).astype(BF16), g2_ref[...])

    shp = (tt, bsz, d)
    r_out[...] = r.reshape(shp)
    w_out[...] = decay.reshape(shp)
    k_out[...] = (k * (1.0 + (a - 1.0) * ka_ref[...])).reshape(shp)
    v_out[...] = v.reshape(shp)
    kk = k * kk_ref[...]
    norm = jnp.concatenate([jnp.sqrt(_head_dot(kk[:, c * LANES:(c + 1) * LANES] ** 2))
                            for c in range(HEAD_PAIRS)], axis=1)
    kk = kk / jnp.maximum(norm, 1e-12)
    na_out[...] = (-kk).reshape(shp)
    b_out[...] = (kk * a).reshape(shp)
    g_out[...] = g.reshape(shp)


def _rwkv_in(x, mods, layer, shift0, weights, vres):
    t_len, bsz, d = x.shape
    tt = _time_tile(t_len, bsz, RWKV_IN_ROW_TILE)
    xspec = pl.BlockSpec((tt, bsz, d), lambda i: (i, 0, 0))

    def full(a):
        return _resident(a.shape, lambda i, n=a.ndim: (0,) * n)

    args = [shift0] + list(weights)
    in_specs = [xspec, _mod_spec(bsz, layer, 3), _mod_spec(bsz, layer, 4)] + [full(a) for a in args]
    if vres is not None:
        v0, v1, v2, v_first = vres
        args += [v0, v1, v2, v_first]
        in_specs += [full(v0), full(v1), full(v2), xspec]
    act = jax.ShapeDtypeStruct(x.shape, F32)
    return pl.pallas_call(
        functools.partial(_rwkv_in_kernel, has_vres=vres is not None),
        grid=(t_len // tt,),
        in_specs=in_specs,
        out_specs=[xspec] * 7 + [pl.BlockSpec((bsz, d), lambda i: (0, 0))],
        out_shape=[act] * 7 + [jax.ShapeDtypeStruct((bsz, d), F32)],
        compiler_params=_params("arbitrary"),
        name="rwkv_in",
    )(x, mods, mods, *args)


def _wkv_kernel(*refs, has_state, n_prev):
    r_ref, w_ref, k_ref, v_ref, a_ref, b_ref = refs[:6]
    refs = list(refs[6:])
    s0_ref = refs.pop(0) if has_state else None
    prev_ref = refs.pop(0) if n_prev else None
    y_ref, s_out_ref, s_scr = refs
    tb, bb = r_ref.shape[0], r_ref.shape[1]

    @pl.when(pl.program_id(1) == 0)
    def _():
        if has_state:
            def pack(b, carry):
                for p in range(HEAD_PAIRS):
                    s_scr[b, p] = jnp.concatenate([s0_ref[b, 2 * p], s0_ref[b, 2 * p + 1]], axis=1)
                return carry
            lax.fori_loop(0, bb, pack, 0)
        else:
            s_scr[...] = jnp.zeros_like(s_scr)

    s_ref = s_scr
    head0 = lax.broadcasted_iota(jnp.int32, (SUBLANES, LANES), 1) < HEAD_SIZE
    rows_s = lax.broadcasted_iota(jnp.int32, (WKV_STATE_COLS, LANES), 0)
    rows_y = lax.broadcasted_iota(jnp.int32, (WKV_COLS, 2 * LANES), 0)
    seq_half = lax.broadcasted_iota(jnp.int32, (WKV_COLS, 2 * LANES), 1) // LANES
    zero8 = jnp.zeros((SUBLANES, LANES), F32)
    zero_rows = jnp.zeros((WKV_STATE_COLS, HEAD_SIZE), F32)
    contract_lanes = (((1,), (1,)), ((), ()))

    def split(x):
        return [jnp.where(head0, x, 0.0), jnp.where(head0, 0.0, x)]

    head_dot = _head_dot

    def halves(x):
        return [x[:, :HEAD_SIZE], x[:, HEAD_SIZE:]]

    def load(ref, t, b0, p):
        return ref[t, pl.ds(b0, SUBLANES), pl.ds(p * LANES, LANES)]

    def stage_a(c, b0, p):
        t1, t2 = 2 * c, 2 * c + 1
        w1 = load(w_ref, t1, b0, p)
        vectors = [load(a_ref, t1, b0, p), load(a_ref, t2, b0, p) * w1, w1 * load(r_ref, t1, b0, p),
                   w1 * load(w_ref, t2, b0, p) * load(r_ref, t2, b0, p)]
        lmat = jnp.concatenate([h for x in vectors for h in split(x)], axis=0).astype(BF16)
        none = jnp.zeros_like(lmat)
        proj = None
        for i in range(SUBLANES):
            l_i = jnp.where(rows_s % SUBLANES == i, lmat, none)
            d = lax.dot_general(s_ref[b0 + i, p].astype(BF16), l_i, contract_lanes,
                                preferred_element_type=F32)
            proj = d if proj is None else proj + d
        return proj

    def stage_b(c, b0, p, proj):
        t1, t2 = 2 * c, 2 * c + 1
        r1, w1, k1, v1, b1 = (load(ref, t1, b0, p) for ref in (r_ref, w_ref, k_ref, v_ref, b_ref))
        r2, w2, k2, v2, a2, b2 = (load(ref, t2, b0, p) for ref in (r_ref, w_ref, k_ref, v_ref, a_ref, b_ref))
        w12 = w1 * w2
        b1a2 = head_dot(b1 * a2)
        k1a2 = head_dot(k1 * a2)
        ymat = jnp.concatenate(split(b1 * w2 + b1a2 * b2) + split(b2) + [zero8] * 4
                               + split(k1 * w2 + k1a2 * b2) + split(k2), axis=0).astype(BF16)
        ymat = jnp.concatenate([ymat, ymat], axis=1)
        none = jnp.zeros_like(ymat)
        v_rows = jnp.concatenate([zero_rows] + halves(v1) + halves(v2), axis=0)
        pb = jnp.concatenate([proj, v_rows.T[:, WKV_STATE_COLS:]], axis=1).astype(BF16)
        for i in range(0, SUBLANES, 2):
            upd = _dot(pb, jnp.where(rows_y % SUBLANES == i + seq_half, ymat, none))
            s_ref[b0 + i, p] = s_ref[b0 + i, p] * w12[i:i + 1] + upd[:, :LANES]
            s_ref[b0 + i + 1, p] = s_ref[b0 + i + 1, p] * w12[i + 1:i + 2] + upd[:, LANES:]
        pt = proj.T

        def rows_of(j):
            lo = 2 * j * SUBLANES
            return jnp.concatenate([pt[lo:lo + SUBLANES], pt[lo + SUBLANES:lo + 2 * SUBLANES]], axis=1)

        u1, s0a2, s0r1, s0r2 = rows_of(0), rows_of(1), rows_of(2), rows_of(3)
        b2r2 = head_dot(b2 * r2)
        w2r2 = w2 * r2
        y_ref[t1, pl.ds(b0, SUBLANES), pl.ds(p * LANES, LANES)] = (
            s0r1 + u1 * head_dot(b1 * r1) + v1 * head_dot(k1 * r1))
        y_ref[t2, pl.ds(b0, SUBLANES), pl.ds(p * LANES, LANES)] = (
            s0r2 + u1 * (head_dot(b1 * w2r2) + b1a2 * b2r2) + v1 * (head_dot(k1 * w2r2) + k1a2 * b2r2)
            + s0a2 * b2r2 + v2 * head_dot(k2 * r2))

    tiles = [(g * SUBLANES, p) for g in range(bb // SUBLANES) for p in range(HEAD_PAIRS)]
    n_chunks = tb // 2
    unroll = max(u for u in range(1, WKV_CHUNK_UNROLL + 1) if n_chunks % u == 0)

    def t_body(cu, carry):
        work = [(cu * unroll + u, b0, p) for u in range(unroll) for (b0, p) in tiles]
        pending = {}
        for step in range(len(work) + WKV_LAG):
            if step < len(work):
                pending[step] = stage_a(*work[step])
            if step >= WKV_LAG:
                stage_b(*work[step - WKV_LAG], pending.pop(step - WKV_LAG))
        return carry

    lax.fori_loop(0, n_chunks // unroll, t_body, 0)

    @pl.when(pl.program_id(1) == pl.num_programs(1) - 1)
    def _():
        def unpack(b, carry):
            for n in range(n_prev):
                s_out_ref[n, b] = prev_ref[n, b]
            for p in range(HEAD_PAIRS):
                s = s_scr[b, p]
                s_out_ref[n_prev, b, 2 * p] = s[:, :HEAD_SIZE]
                s_out_ref[n_prev, b, 2 * p + 1] = s[:, HEAD_SIZE:]
            return carry
        lax.fori_loop(0, bb, unpack, 0)


def _wkv(r, w, k, v, a, b, s0_all, layer_idx, prev_states):
    t_len, bsz, d = r.shape
    bb = min(bsz, WKV_BATCH_TILE)
    tb = min(t_len, WKV_TIME_TILE)
    assert tb % 2 == 0 and t_len % tb == 0 and bsz % bb == 0
    state = (RWKV_HEADS, HEAD_SIZE, HEAD_SIZE)
    xspec = pl.BlockSpec((tb, bb, d), lambda i, j: (j, i, 0))
    has_state = s0_all is not None
    n_prev = 0 if prev_states is None else prev_states.shape[0]
    extra, extra_specs = [], []
    if has_state:
        extra.append(s0_all)
        extra_specs.append(pl.BlockSpec((None, bb) + state, lambda i, j: (layer_idx, i, 0, 0, 0)))
    if n_prev:
        extra.append(prev_states)
        extra_specs.append(pl.BlockSpec((n_prev, bb) + state, lambda i, j: (0, i, 0, 0, 0)))
    return pl.pallas_call(
        functools.partial(_wkv_kernel, has_state=has_state, n_prev=n_prev),
        grid=(bsz // bb, t_len // tb),
        in_specs=[xspec] * 6 + extra_specs,
        out_specs=[xspec, pl.BlockSpec((n_prev + 1, bb) + state, lambda i, j: (0, i, 0, 0, 0))],
        out_shape=[jax.ShapeDtypeStruct(r.shape, F32),
                   jax.ShapeDtypeStruct((n_prev + 1, bsz) + state, F32)],
        scratch_shapes=[pltpu.VMEM((bb, HEAD_PAIRS, HEAD_SIZE, LANES), F32)],
        compiler_params=_params("parallel", "arbitrary"),
        name="wkv",
    )(r, w, k, v, a, b, *extra)


def _rwkv_out_kernel(y_ref, r_ref, k_ref, v_ref, g_ref, x_ref, gate_ref, lnw_ref, lnb_ref, rk_ref,
                     wo_ref, o_ref):
    x = x_ref[...]
    tt, bsz, d = x.shape
    tm = tt * bsz
    ones = _head_ones()
    y = y_ref[...].reshape(tm, d)
    yc = y - _head_sum(y, ones) * (1.0 / HEAD_SIZE)
    var = _head_sum(yc * yc, ones) * (1.0 / HEAD_SIZE)
    yn = yc * lax.rsqrt(var + GN_EPS) * lnw_ref[...] + lnb_ref[...]
    rk = r_ref[...].reshape(tm, d) * k_ref[...].reshape(tm, d) * rk_ref[...]
    bonus = _head_sum(rk, ones) * v_ref[...].reshape(tm, d)
    out = _dot(((yn + bonus) * g_ref[...].reshape(tm, d)).astype(BF16), wo_ref[...])
    o_ref[...] = x + (1.0 + gate_ref[...])[None] * out.reshape(tt, bsz, d)


def _rwkv_out(y, r, k, v, g, x, mods, layer, ln_w, ln_b, r_k, w_o):
    t_len, bsz, d = x.shape
    tt = _time_tile(t_len, bsz)
    xspec = pl.BlockSpec((tt, bsz, d), lambda i: (i, 0, 0))

    def full(a):
        return _resident(a.shape, lambda i, n=a.ndim: (0,) * n)

    return pl.pallas_call(
        _rwkv_out_kernel,
        grid=(t_len // tt,),
        in_specs=[xspec] * 6 + [_mod_spec(bsz, layer, 5)] + [full(a) for a in (ln_w, ln_b, r_k, w_o)],
        out_specs=xspec,
        out_shape=jax.ShapeDtypeStruct(x.shape, F32),
        compiler_params=_params("parallel"),
        name="rwkv_out",
    )(y, r, k, v, g, x, mods, ln_w, ln_b, r_k, w_o)


def _final_kernel(x_ref, gain_ref, o_ref, *, batch_major_out):
    x = x_ref[...]
    ms = jnp.mean(x * x, axis=-1, keepdims=True)
    y = (x * lax.rsqrt(ms + NORM_EPS)) * gain_ref[...][None]
    o_ref[...] = jnp.swapaxes(y, 0, 1) if batch_major_out else y


def _batch_major_tiles(t_len, bsz):
    return _time_tile(t_len, bsz) % SUBLANES == 0


def _final(x, gain, batch_major_out):
    t_len, bsz, d = x.shape
    tt = _time_tile(t_len, bsz)
    xspec = pl.BlockSpec((tt, bsz, d), lambda i: (i, 0, 0))
    return pl.pallas_call(
        functools.partial(_final_kernel, batch_major_out=batch_major_out),
        grid=(t_len // tt,),
        in_specs=[xspec, pl.BlockSpec((1, d), lambda i: (0, 0))],
        out_specs=pl.BlockSpec((bsz, tt, d), lambda i: (0, i, 0)) if batch_major_out else xspec,
        out_shape=jax.ShapeDtypeStruct((bsz, t_len, d) if batch_major_out else x.shape, F32),
        compiler_params=_params("parallel"),
        name="final_norm",
    )(x, gain)


def _pad_cols(w):
    n = w.shape[-1]
    return jnp.pad(w, [(0, 0)] * (w.ndim - 1) + [(0, -n % LORA_PAD)])


def _pad_rows(w):
    n = w.shape[-2]
    return jnp.pad(w, [(0, 0)] * (w.ndim - 2) + [(0, -n % LORA_PAD), (0, 0)])


def _trunk(x, mods, lru_h0, lru_conv0, rwkv_shift0, rwkv_wkv0, W):
    new_h, new_conv, new_shift, new_wkv = [], [], [], None
    v_first = None
    fold = _batch_major_tiles(x.shape[1], x.shape[0])
    if not fold:
        x = jnp.transpose(x, (1, 0, 2))
    for layer in range(DEPTH):
        j = layer // 2
        x = _ffn(x, mods, layer, 0, W['ffn_w_in'], W['ffn_w_out'], batch_major_in=fold and layer == 0)
        if layer % 2 == 0:
            x, h_last, buf = _lru(x, mods, layer, lru_h0[j], lru_conv0[j], W['lru_w_in'][j],
                                  W['lru_conv_w'][j], W['lru_conv_b'][j], W['lru_gate_w'][j],
                                  W['lru_gate_b'][j], W['lru_lambda'][j], W['lru_w_out'][j])
            new_h.append(h_last)
            new_conv.append(buf)
        else:
            weights = [W[n][j] for n in ('rwkv_mu', 'rwkv_w_rkv', 'rwkv_w0', 'rwkv_w1', 'rwkv_w2',
                                         'rwkv_a0', 'rwkv_a1', 'rwkv_a2', 'rwkv_g1', 'rwkv_g2',
                                         'rwkv_k_k', 'rwkv_k_a')]
            vres = None if j == 0 else (W['rwkv_v0'][j - 1], W['rwkv_v1'][j - 1], W['rwkv_v2'][j - 1],
                                        v_first)
            r, w, k, v, na, b, g, shift_last = _rwkv_in(x, mods, layer, rwkv_shift0[j], weights, vres)
            if v_first is None:
                v_first = v
            y, new_wkv = _wkv(r, w, k, v, na, b, rwkv_wkv0, j, new_wkv)
            x = _rwkv_out(y, r, k, v, g, x, mods, layer, W['rwkv_ln_w'][j], W['rwkv_ln_b'][j],
                          W['rwkv_r_k'][j], W['rwkv_w_o'][j])
            new_shift.append(shift_last)
        x = _ffn(x, mods, layer, 1, W['ffn_w_in'], W['ffn_w_out'])
    y = _final(x, W['final_gain'], batch_major_out=fold)
    if not fold:
        y = jnp.transpose(y, (1, 0, 2))
    return y, jnp.stack(new_h), jnp.stack(new_conv), jnp.stack(new_shift), new_wkv


def _prep_weights(P):
    row = lambda a: a.reshape(a.shape[0], 1, -1)
    W = {
        'ffn_w_in': P['ffn_w_in'],
        'ffn_w_out': P['ffn_w_out'],
        'lru_w_in': P['lru_w_in'].astype(BF16),
        'lru_conv_w': P['lru_conv_w'],
        'lru_conv_b': row(P['lru_conv_b']),
        'lru_gate_w': P['lru_gate_w'].astype(BF16),
        'lru_gate_b': P['lru_gate_b'],
        'lru_lambda': row(P['lru_lambda']),
        'lru_w_out': P['lru_w_out'].astype(BF16),
        'rwkv_mu': P['rwkv_mu'],
        'rwkv_w_rkv': P['rwkv_w_rkv'].astype(BF16),
        'rwkv_w_o': P['rwkv_w_o'].astype(BF16),
        'rwkv_r_k': row(P['rwkv_r_k']),
        'final_gain': P['final_gain'].reshape(1, -1),
    }
    for n in ('rwkv_w0', 'rwkv_a0', 'rwkv_v0', 'rwkv_k_k', 'rwkv_k_a', 'rwkv_ln_w', 'rwkv_ln_b'):
        W[n] = row(P[n])
    for n in ('rwkv_w1', 'rwkv_a1', 'rwkv_v1', 'rwkv_g1'):
        W[n] = _pad_cols(P[n]).astype(BF16)
    for n in ('rwkv_w2', 'rwkv_a2', 'rwkv_v2', 'rwkv_g2'):
        W[n] = _pad_rows(P[n]).astype(BF16)
    return W


def _run_stream(x, mods, lru_h, lru_conv, rwkv_shift, rwkv_wkv, W):
    y, h, conv, shift, wkv = _trunk(x, mods, lru_h, jnp.transpose(lru_conv, (0, 2, 1, 3)), rwkv_shift,
                                    rwkv_wkv, W)
    return y, h, jnp.transpose(conv, (0, 2, 1, 3)), shift, wkv


def kernel(x_prompt, x_sample, state_lru_h, state_lru_conv, state_rwkv_shift, state_rwkv_wkv, c_prompt, c_sample, ada_w, ada_b, ffn_w_in, ffn_w_out, lru_w_in, lru_conv_w, lru_conv_b, lru_gate_w, lru_gate_b, lru_lambda, lru_w_out, rwkv_mu, rwkv_w_rkv, rwkv_w_o, rwkv_w0, rwkv_w1, rwkv_w2, rwkv_a0, rwkv_a1, rwkv_a2, rwkv_v0, rwkv_v1, rwkv_v2, rwkv_g1, rwkv_g2, rwkv_k_k, rwkv_k_a, rwkv_r_k, rwkv_ln_w, rwkv_ln_b, final_gain):
    P = dict(ffn_w_in=ffn_w_in, ffn_w_out=ffn_w_out,
             lru_w_in=lru_w_in, lru_conv_w=lru_conv_w, lru_conv_b=lru_conv_b, lru_gate_w=lru_gate_w,
             lru_gate_b=lru_gate_b, lru_lambda=lru_lambda, lru_w_out=lru_w_out,
             rwkv_mu=rwkv_mu, rwkv_w_rkv=rwkv_w_rkv, rwkv_w_o=rwkv_w_o, rwkv_w0=rwkv_w0, rwkv_w1=rwkv_w1,
             rwkv_w2=rwkv_w2, rwkv_a0=rwkv_a0, rwkv_a1=rwkv_a1, rwkv_a2=rwkv_a2, rwkv_v0=rwkv_v0,
             rwkv_v1=rwkv_v1, rwkv_v2=rwkv_v2, rwkv_g1=rwkv_g1, rwkv_g2=rwkv_g2, rwkv_k_k=rwkv_k_k,
             rwkv_k_a=rwkv_k_a, rwkv_r_k=rwkv_r_k, rwkv_ln_w=rwkv_ln_w, rwkv_ln_b=rwkv_ln_b,
             final_gain=final_gain)
    W = _prep_weights(P)
    bp, bs = x_prompt.shape[0], x_sample.shape[0]
    mods = _ada(jnp.concatenate([c_prompt, c_sample], axis=0), ada_w, ada_b)
    n_lru, n_rwkv = state_lru_h.shape[0], state_rwkv_shift.shape[0]
    zeros = lambda *s: jnp.zeros(s, F32)
    out_p = _run_stream(x_prompt, mods[:, :bp], zeros(n_lru, bp, D_RNN), zeros(n_lru, bp, CONV_W - 1, D_RNN),
                        zeros(n_rwkv, bp, D_MODEL), None, W)
    out_s = _run_stream(x_sample, mods[:, bp:], state_lru_h, state_lru_conv, state_rwkv_shift,
                        state_rwkv_wkv, W)
    return (out_p[0], out_s[0]) + out_p[1:] + out_s[1:]
```

```python
import functools

import jax
import jax.numpy as jnp
from jax import lax
from jax.experimental import pallas as pl
from jax.experimental.pallas import tpu as pltpu

D_MODEL = 1024
DEPTH = 4
D_RNN = D_MODEL
LRU_BLOCKS = 4
LRU_BLOCK_W = D_RNN // LRU_BLOCKS
CONV_W = 4
RG_C = 8.0
HEAD_SIZE = 64
RWKV_HEADS = D_MODEL // HEAD_SIZE
D_FF = 2816
FFN_RES = 0.5
N_MOD = 9
NORM_EPS = 1e-6
GN_EPS = HEAD_SIZE * 1e-5
DECAY_SCALE = 0.6065306597126334

LANES = 128
SUBLANES = 8
HEAD_PAIRS = D_MODEL // LANES
LORA_PAD = 128
VMEM_LIMIT_BYTES = 56 * 1024 * 1024
ROW_TILE = 512
RWKV_IN_ROW_TILE = 256
FFN_CHUNK = 256
ADA_TILE = 1152
SCAN_CARRY_VREGS = 16
WKV_STATE_COLS = 8 * SUBLANES
WKV_COLS = 12 * SUBLANES
WKV_BATCH_TILE = 8
WKV_TIME_TILE = 32
WKV_CHUNK_UNROLL = 4
WKV_LAG = 3

F32 = jnp.float32
BF16 = jnp.bfloat16


def _dot(a, b):
    return jnp.dot(a, b, preferred_element_type=F32)


def _params(*sem):
    return pltpu.CompilerParams(dimension_semantics=sem, vmem_limit_bytes=VMEM_LIMIT_BYTES)


def _resident(block_shape, index_map):
    return pl.BlockSpec(block_shape, index_map, pipeline_mode=pl.Buffered(1))


def _rms_mod(x, shift, scale):
    ms = jnp.mean(x * x, axis=-1, keepdims=True)
    return (x * lax.rsqrt(ms + NORM_EPS)) * (1.0 + scale)[None] + shift[None]


def _softplus(x):
    return jnp.maximum(x, 0.0) + jnp.log1p(jnp.exp(-jnp.abs(x)))


def _head_ones():
    r = lax.broadcasted_iota(jnp.int32, (LANES, LANES), 0) // HEAD_SIZE
    c = lax.broadcasted_iota(jnp.int32, (LANES, LANES), 1) // HEAD_SIZE
    return (r == c).astype(BF16)


def _head_sum(x, ones):
    outs = []
    for c in range(HEAD_PAIRS):
        xc = x[:, c * LANES:(c + 1) * LANES]
        hi = xc.astype(BF16)
        lo = (xc - hi.astype(F32)).astype(BF16)
        outs.append(_dot(hi, ones) + _dot(lo, ones))
    return jnp.concatenate(outs, axis=-1)


def _head_dot(x):
    head0 = lax.broadcasted_iota(jnp.int32, x.shape, 1) < HEAD_SIZE
    s0 = jnp.sum(jnp.where(head0, x, 0.0), axis=1, keepdims=True)
    s1 = jnp.sum(jnp.where(head0, 0.0, x), axis=1, keepdims=True)
    return jnp.where(head0, s0, s1)


def _ada_kernel(c_ref, w_ref, b_ref, o_ref):
    c = c_ref[...]
    s = (c * jax.nn.sigmoid(c)).astype(BF16)
    o_ref[...] = _dot(s, w_ref[...].astype(BF16)) + b_ref[...]


def _ada(c_all, ada_w, ada_b):
    n = c_all.shape[0]
    nd = N_MOD * D_MODEL
    return pl.pallas_call(
        _ada_kernel,
        grid=(DEPTH, nd // ADA_TILE),
        in_specs=[
            pl.BlockSpec((n, D_MODEL), lambda l, j: (0, 0)),
            pl.BlockSpec((None, D_MODEL, ADA_TILE), lambda l, j: (l, 0, j)),
            pl.BlockSpec((None, 1, ADA_TILE), lambda l, j: (l, 0, j)),
        ],
        out_specs=pl.BlockSpec((None, n, ADA_TILE), lambda l, j: (l, 0, j)),
        out_shape=jax.ShapeDtypeStruct((DEPTH, n, nd), F32),
        compiler_params=_params("parallel", "parallel"),
        name="ada_mod",
    )(c_all, ada_w, ada_b.reshape(DEPTH, 1, nd))


def _mod_spec(bsz, layer, m):
    return pl.BlockSpec((None, bsz, D_MODEL), lambda *_: (layer, 0, m))


def _time_tile(t_len, bsz, rows=ROW_TILE):
    return max(1, min(t_len, rows // bsz))


def _ffn_kernel(x_ref, sh_ref, sc_ref, g_ref, win_ref, wout_ref, o_ref, *, batch_major_in):
    x = x_ref[...]
    if batch_major_in:
        x = jnp.swapaxes(x, 0, 1)
    tt, bsz, d = x.shape
    h = _rms_mod(x, sh_ref[...], sc_ref[...]).reshape(tt * bsz, d).astype(BF16)
    acc = jnp.zeros((tt * bsz, d), F32)
    for j in range(D_FF // FFN_CHUNK):
        lo = j * FFN_CHUNK
        gate = _dot(h, win_ref[:, lo:lo + FFN_CHUNK].astype(BF16))
        up = _dot(h, win_ref[:, D_FF + lo:D_FF + lo + FFN_CHUNK].astype(BF16))
        act = (gate * jax.nn.sigmoid(gate) * up).astype(BF16)
        acc = acc + _dot(act, wout_ref[lo:lo + FFN_CHUNK, :].astype(BF16))
    o_ref[...] = x + (FFN_RES * (1.0 + g_ref[...]))[None] * acc.reshape(tt, bsz, d)


def _ffn(x, mods, layer, sub, w_in, w_out, batch_major_in=False):
    if batch_major_in:
        bsz, t_len, d = x.shape
    else:
        t_len, bsz, d = x.shape
    tt = _time_tile(t_len, bsz)
    m0 = 0 if sub == 0 else 6
    xspec = pl.BlockSpec((tt, bsz, d), lambda i: (i, 0, 0))
    return pl.pallas_call(
        functools.partial(_ffn_kernel, batch_major_in=batch_major_in),
        grid=(t_len // tt,),
        in_specs=[
            pl.BlockSpec((bsz, tt, d), lambda i: (0, i, 0)) if batch_major_in else xspec,
            _mod_spec(bsz, layer, m0), _mod_spec(bsz, layer, m0 + 1), _mod_spec(bsz, layer, m0 + 2),
            _resident((None, None, d, 2 * D_FF), lambda i: (layer, sub, 0, 0)),
            _resident((None, None, D_FF, d), lambda i: (layer, sub, 0, 0)),
        ],
        out_specs=xspec,
        out_shape=jax.ShapeDtypeStruct((t_len, bsz, d), F32),
        compiler_params=_params("parallel"),
        name="ffn",
    )(x, mods, mods, mods, w_in, w_out)


def _lru_kernel(x_ref, sh_ref, sc_ref, g_ref, h0_ref, conv0_ref, win_ref, cw_ref, cb_ref, gw_ref,
                gb_ref, lam_ref, wout_ref, o_ref, hlast_ref, buf_ref, a_scr, u_scr):
    @pl.when(pl.program_id(0) == 0)
    def _():
        hlast_ref[...] = h0_ref[...]
        buf_ref[...] = conv0_ref[...]

    x = x_ref[...]
    tt, bsz, d = x.shape
    tm = tt * bsz
    h = _rms_mod(x, sh_ref[...], sc_ref[...]).reshape(tm, d).astype(BF16)
    proj = _dot(h, win_ref[...])
    gate_branch = proj[:, :d]
    xp = jnp.concatenate([buf_ref[...], proj[:, d:].reshape(tt, bsz, d)], axis=0)
    cw = cw_ref[...]
    xc = cb_ref[...][None]
    for j in range(CONV_W):
        xc = xc + xp[j:j + tt] * cw[j:j + 1][None]
    buf_ref[...] = xp[tt:tt + CONV_W - 1]
    xc = xc.reshape(tm, d)

    lam = lam_ref[...]
    log_sig_lam = -_softplus(-lam)
    for n in range(LRU_BLOCKS):
        sl = slice(n * LRU_BLOCK_W, (n + 1) * LRU_BLOCK_W)
        xb = xc[:, sl]
        gates = _dot(xb.astype(BF16), gw_ref[n]) + gb_ref[n:n + 1, :]
        r = jax.nn.sigmoid(gates[:, :LRU_BLOCK_W])
        ig = jax.nn.sigmoid(gates[:, LRU_BLOCK_W:])
        log_a = RG_C * r * log_sig_lam[:, sl]
        a_scr[:, sl] = jnp.exp(log_a)
        th = jnp.tanh(log_a)
        u_scr[:, sl] = jnp.sqrt(-2.0 * th / (1.0 - th)) * (ig * xb)

    cwid = min(d, SCAN_CARRY_VREGS * SUBLANES * LANES // bsz)
    for c in range(d // cwid):
        ls = slice(c * cwid, (c + 1) * cwid)

        def step(t, hc, ls=ls):
            rows = pl.ds(pl.multiple_of(t * bsz, bsz), bsz)
            hc = a_scr[rows, ls] * hc + u_scr[rows, ls]
            u_scr[rows, ls] = hc
            return hc

        hlast_ref[:, ls] = lax.fori_loop(0, tt, step, hlast_ref[:, ls])

    gelu = 0.5 * gate_branch * (1.0 + jnp.tanh(
        0.7978845608028654 * (gate_branch + 0.044715 * (gate_branch * gate_branch * gate_branch))))
    y = _dot((u_scr[...] * gelu).astype(BF16), wout_ref[...])
    o_ref[...] = x + (1.0 + g_ref[...])[None] * y.reshape(tt, bsz, d)


def _lru(x, mods, layer, h0, conv0, w_in, conv_w, conv_b, gate_w, gate_b, lam, w_out):
    t_len, bsz, d = x.shape
    tt = _time_tile(t_len, bsz)
    xspec = pl.BlockSpec((tt, bsz, d), lambda i: (i, 0, 0))

    def full(a):
        return _resident(a.shape, lambda i, n=a.ndim: (0,) * n)

    args = (h0, conv0, w_in, conv_w, conv_b, gate_w, gate_b, lam, w_out)
    return pl.pallas_call(
        _lru_kernel,
        grid=(t_len // tt,),
        in_specs=[xspec, _mod_spec(bsz, layer, 3), _mod_spec(bsz, layer, 4), _mod_spec(bsz, layer, 5)]
        + [full(a) for a in args],
        out_specs=[xspec,
                   pl.BlockSpec((bsz, d), lambda i: (0, 0)),
                   pl.BlockSpec((CONV_W - 1, bsz, d), lambda i: (0, 0, 0))],
        out_shape=[jax.ShapeDtypeStruct(x.shape, F32),
                   jax.ShapeDtypeStruct((bsz, d), F32),
                   jax.ShapeDtypeStruct((CONV_W - 1, bsz, d), F32)],
        scratch_shapes=[pltpu.VMEM((tt * bsz, d), F32), pltpu.VMEM((tt * bsz, d), F32)],
        compiler_params=_params("arbitrary"),
        name="rglru",
    )(x, mods, mods, mods, *args)


def _rwkv_in_kernel(*refs, has_vres):
    (x_ref, sh_ref, sc_ref, shift0_ref, mu_ref, wrkv_ref, w0_ref, w1_ref, w2_ref, a0_ref, a1_ref,
     a2_ref, g1_ref, g2_ref, kk_ref, ka_ref) = refs[:16]
    refs = refs[16:]
    if has_vres:
        v0_ref, v1_ref, v2_ref, vfirst_ref = refs[:4]
        refs = refs[4:]
    r_out, w_out, k_out, v_out, na_out, b_out, g_out, shift_ref = refs

    @pl.when(pl.program_id(0) == 0)
    def _():
        shift_ref[...] = shift0_ref[...]

    x = x_ref[...]
    tt, bsz, d = x.shape
    tm = tt * bsz
    xf = _rms_mod(x, sh_ref[...], sc_ref[...])
    prev = jnp.concatenate([shift_ref[...][None], xf[:tt - 1]], axis=0)
    shift_ref[...] = xf[tt - 1]
    xf_b = xf.reshape(tm, d).astype(BF16)
    dx_b = (prev - xf).reshape(tm, d).astype(BF16)

    def mix(j):
        return xf_b + dx_b * mu_ref[j:j + 1, :].astype(BF16)

    r = _dot(mix(0), wrkv_ref[0])
    k = _dot(mix(1), wrkv_ref[1])
    xm_v = mix(2)
    v = _dot(xm_v, wrkv_ref[2])
    z = w0_ref[...] + _dot(jnp.tanh(_dot(mix(3), w1_ref[...])).astype(BF16), w2_ref[...])
    decay = jnp.exp(-DECAY_SCALE * jax.nn.sigmoid(z))
    if has_vres:
        mix_v = jax.nn.sigmoid(v0_ref[...] + _dot(_dot(xm_v, v1_ref[...]).astype(BF16), v2_ref[...]))
        v = v + (vfirst_ref[...].reshape(tm, d) - v) * mix_v
    a = jax.nn.sigmoid(a0_ref[...] + _dot(_dot(mix(4), a1_ref[...]).astype(BF16), a2_ref[...]))
    g = _dot(jax.nn.sigmoid(_dot(mix(5), g1_ref[...])).astype(BF16), g2_ref[...])

    shp = (tt, bsz, d)
    r_out[...] = r.reshape(shp)
    w_out[...] = decay.reshape(shp)
    k_out[...] = (k * (1.0 + (a - 1.0) * ka_ref[...])).reshape(shp)
    v_out[...] = v.reshape(shp)
    kk = k * kk_ref[...]
    norm = jnp.concatenate([jnp.sqrt(_head_dot(kk[:, c * LANES:(c + 1) * LANES] ** 2))
                            for c in range(HEAD_PAIRS)], axis=1)
    kk = kk / jnp.maximum(norm, 1e-12)
    na_out[...] = (-kk).reshape(shp)
    b_out[...] = (kk * a).reshape(shp)
    g_out[...] = g.reshape(shp)


def _rwkv_in(x, mods, layer, shift0, weights, vres):
    t_len, bsz, d = x.shape
    tt = _time_tile(t_len, bsz, RWKV_IN_ROW_TILE)
    xspec = pl.BlockSpec((tt, bsz, d), lambda i: (i, 0, 0))

    def full(a):
        return _resident(a.shape, lambda i, n=a.ndim: (0,) * n)

    args = [shift0] + list(weights)
    in_specs = [xspec, _mod_spec(bsz, layer, 3), _mod_spec(bsz, layer, 4)] + [full(a) for a in args]
    if vres is not None:
        v0, v1, v2, v_first = vres
        args += [v0, v1, v2, v_first]
        in_specs += [full(v0), full(v1), full(v2), xspec]
    act = jax.ShapeDtypeStruct(x.shape, F32)
    return pl.pallas_call(
        functools.partial(_rwkv_in_kernel, has_vres=vres is not None),
        grid=(t_len // tt,),
        in_specs=in_specs,
        out_specs=[xspec] * 7 + [pl.BlockSpec((bsz, d), lambda i: (0, 0))],
        out_shape=[act] * 7 + [jax.ShapeDtypeStruct((bsz, d), F32)],
        compiler_params=_params("arbitrary"),
        name="rwkv_in",
    )(x, mods, mods, *args)


def _wkv_kernel(*refs, has_state, n_prev):
    r_ref, w_ref, k_ref, v_ref, a_ref, b_ref = refs[:6]
    refs = list(refs[6:])
    s0_ref = refs.pop(0) if has_state else None
    prev_ref = refs.pop(0) if n_prev else None
    y_ref, s_out_ref, s_scr = refs
    tb, bb = r_ref.shape[0], r_ref.shape[1]

    @pl.when(pl.program_id(1) == 0)
    def _():
        if has_state:
            def pack(b, carry):
                for p in range(HEAD_PAIRS):
                    s_scr[b, p] = jnp.concatenate([s0_ref[b, 2 * p], s0_ref[b, 2 * p + 1]], axis=1)
                return carry
            lax.fori_loop(0, bb, pack, 0)
        else:
            s_scr[...] = jnp.zeros_like(s_scr)

    s_ref = s_scr
    head0 = lax.broadcasted_iota(jnp.int32, (SUBLANES, LANES), 1) < HEAD_SIZE
    rows_s = lax.broadcasted_iota(jnp.int32, (WKV_STATE_COLS, LANES), 0)
    rows_y = lax.broadcasted_iota(jnp.int32, (WKV_COLS, 2 * LANES), 0)
    seq_half = lax.broadcasted_iota(jnp.int32, (WKV_COLS, 2 * LANES), 1) // LANES
    zero8 = jnp.zeros((SUBLANES, LANES), F32)
    zero_rows = jnp.zeros((WKV_STATE_COLS, HEAD_SIZE), F32)
    contract_lanes = (((1,), (1,)), ((), ()))

    def split(x):
        return [jnp.where(head0, x, 0.0), jnp.where(head0, 0.0, x)]

    head_dot = _head_dot

    def halves(x):
        return [x[:, :HEAD_SIZE], x[:, HEAD_SIZE:]]

    def load(ref, t, b0, p):
        return ref[t, pl.ds(b0, SUBLANES), pl.ds(p * LANES, LANES)]

    def stage_a(c, b0, p):
        t1, t2 = 2 * c, 2 * c + 1
        w1 = load(w_ref, t1, b0, p)
        vectors = [load(a_ref, t1, b0, p), load(a_ref, t2, b0, p) * w1, w1 * load(r_ref, t1, b0, p),
                   w1 * load(w_ref, t2, b0, p) * load(r_ref, t2, b0, p)]
        lmat = jnp.concatenate([h for x in vectors for h in split(x)], axis=0).astype(BF16)
        none = jnp.zeros_like(lmat)
        proj = None
        for i in range(SUBLANES):
            l_i = jnp.where(rows_s % SUBLANES == i, lmat, none)
            d = lax.dot_general(s_ref[b0 + i, p].astype(BF16), l_i, contract_lanes,
                                preferred_element_type=F32)
            proj = d if proj is None else proj + d
        return proj

    def stage_b(c, b0, p, proj):
        t1, t2 = 2 * c, 2 * c + 1
        r1, w1, k1, v1, b1 = (load(ref, t1, b0, p) for ref in (r_ref, w_ref, k_ref, v_ref, b_ref))
        r2, w2, k2, v2, a2, b2 = (load(ref, t2, b0, p) for ref in (r_ref, w_ref, k_ref, v_ref, a_ref, b_ref))
        w12 = w1 * w2
        b1a2 = head_dot(b1 * a2)
        k1a2 = head_dot(k1 * a2)
        ymat = jnp.concatenate(split(b1 * w2 + b1a2 * b2) + split(b2) + [zero8] * 4
                               + split(k1 * w2 + k1a2 * b2) + split(k2), axis=0).astype(BF16)
        ymat = jnp.concatenate([ymat, ymat], axis=1)
        none = jnp.zeros_like(ymat)
        v_rows = jnp.concatenate([zero_rows] + halves(v1) + halves(v2), axis=0)
        pb = jnp.concatenate([proj, v_rows.T[:, WKV_STATE_COLS:]], axis=1).astype(BF16)
        for i in range(0, SUBLANES, 2):
            upd = _dot(pb, jnp.where(rows_y % SUBLANES == i + seq_half, ymat, none))
            s_ref[b0 + i, p] = s_ref[b0 + i, p] * w12[i:i + 1] + upd[:, :LANES]
            s_ref[b0 + i + 1, p] = s_ref[b0 + i + 1, p] * w12[i + 1:i + 2] + upd[:, LANES:]
        pt = proj.T

        def rows_of(j):
            lo = 2 * j * SUBLANES
            return jnp.concatenate([pt[lo:lo + SUBLANES], pt[lo + SUBLANES:lo + 2 * SUBLANES]], axis=1)

        u1, s0a2, s0r1, s0r2 = rows_of(0), rows_of(1), rows_of(2), rows_of(3)
        b2r2 = head_dot(b2 * r2)
        w2r2 = w2 * r2
        y_ref[t1, pl.ds(b0, SUBLANES), pl.ds(p * LANES, LANES)] = (
            s0r1 + u1 * head_dot(b1 * r1) + v1 * head_dot(k1 * r1))
        y_ref[t2, pl.ds(b0, SUBLANES), pl.ds(p * LANES, LANES)] = (
            s0r2 + u1 * (head_dot(b1 * w2r2) + b1a2 * b2r2) + v1 * (head_dot(k1 * w2r2) + k1a2 * b2r2)
            + s0a2 * b2r2 + v2 * head_dot(k2 * r2))

    tiles = [(g * SUBLANES, p) for g in range(bb // SUBLANES) for p in range(HEAD_PAIRS)]
    n_chunks = tb // 2
    unroll = max(u for u in range(1, WKV_CHUNK_UNROLL + 1) if n_chunks % u == 0)

    def t_body(cu, carry):
        work = [(cu * unroll + u, b0, p) for u in range(unroll) for (b0, p) in tiles]
        pending = {}
        for step in range(len(work) + WKV_LAG):
            if step < len(work):
                pending[step] = stage_a(*work[step])
            if step >= WKV_LAG:
                stage_b(*work[step - WKV_LAG], pending.pop(step - WKV_LAG))
        return carry

    lax.fori_loop(0, n_chunks // unroll, t_body, 0)

    @pl.when(pl.program_id(1) == pl.num_programs(1) - 1)
    def _():
        def unpack(b, carry):
            for n in range(n_prev):
                s_out_ref[n, b] = prev_ref[n, b]
            for p in range(HEAD_PAIRS):
                s = s_scr[b, p]
                s_out_ref[n_prev, b, 2 * p] = s[:, :HEAD_SIZE]
                s_out_ref[n_prev, b, 2 * p + 1] = s[:, HEAD_SIZE:]
            return carry
        lax.fori_loop(0, bb, unpack, 0)


def _wkv(r, w, k, v, a, b, s0, prev_states):
    t_len, bsz, d = r.shape
    bb = min(bsz, WKV_BATCH_TILE)
    tb = min(t_len, WKV_TIME_TILE)
    assert tb % 2 == 0 and t_len % tb == 0 and bsz % bb == 0
    state = (RWKV_HEADS, HEAD_SIZE, HEAD_SIZE)
    xspec = pl.BlockSpec((tb, bb, d), lambda i, j: (j, i, 0))
    has_state = s0 is not None
    n_prev = 0 if prev_states is None else prev_states.shape[0]
    extra, extra_specs = [], []
    if has_state:
        extra.append(s0)
        extra_specs.append(pl.BlockSpec((bb,) + state, lambda i, j: (i, 0, 0, 0)))
    if n_prev:
        extra.append(prev_states)
        extra_specs.append(pl.BlockSpec((n_prev, bb) + state, lambda i, j: (0, i, 0, 0, 0)))
    return pl.pallas_call(
        functools.partial(_wkv_kernel, has_state=has_state, n_prev=n_prev),
        grid=(bsz // bb, t_len // tb),
        in_specs=[xspec] * 6 + extra_specs,
        out_specs=[xspec, pl.BlockSpec((n_prev + 1, bb) + state, lambda i, j: (0, i, 0, 0, 0))],
        out_shape=[jax.ShapeDtypeStruct(r.shape, F32),
                   jax.ShapeDtypeStruct((n_prev + 1, bsz) + state, F32)],
        scratch_shapes=[pltpu.VMEM((bb, HEAD_PAIRS, HEAD_SIZE, LANES), F32)],
        compiler_params=_params("parallel", "arbitrary"),
        name="wkv",
    )(r, w, k, v, a, b, *extra)


def _rwkv_out_kernel(y_ref, r_ref, k_ref, v_ref, g_ref, x_ref, gate_ref, lnw_ref, lnb_ref, rk_ref,
                     wo_ref, o_ref):
    x = x_ref[...]
    tt, bsz, d = x.shape
    tm = tt * bsz
    ones = _head_ones()
    y = y_ref[...].reshape(tm, d)
    yc = y - _head_sum(y, ones) * (1.0 / HEAD_SIZE)
    var = _head_sum(yc * yc, ones) * (1.0 / HEAD_SIZE)
    yn = yc * lax.rsqrt(var + GN_EPS) * lnw_ref[...] + lnb_ref[...]
    rk = r_ref[...].reshape(tm, d) * k_ref[...].reshape(tm, d) * rk_ref[...]
    bonus = _head_sum(rk, ones) * v_ref[...].reshape(tm, d)
    out = _dot(((yn + bonus) * g_ref[...].reshape(tm, d)).astype(BF16), wo_ref[...])
    o_ref[...] = x + (1.0 + gate_ref[...])[None] * out.reshape(tt, bsz, d)


def _rwkv_out(y, r, k, v, g, x, mods, layer, ln_w, ln_b, r_k, w_o):
    t_len, bsz, d = x.shape
    tt = _time_tile(t_len, bsz)
    xspec = pl.BlockSpec((tt, bsz, d), lambda i: (i, 0, 0))

    def full(a):
        return _resident(a.shape, lambda i, n=a.ndim: (0,) * n)

    return pl.pallas_call(
        _rwkv_out_kernel,
        grid=(t_len // tt,),
        in_specs=[xspec] * 6 + [_mod_spec(bsz, layer, 5)] + [full(a) for a in (ln_w, ln_b, r_k, w_o)],
        out_specs=xspec,
        out_shape=jax.ShapeDtypeStruct(x.shape, F32),
        compiler_params=_params("parallel"),
        name="rwkv_out",
    )(y, r, k, v, g, x, mods, ln_w, ln_b, r_k, w_o)


def _final_kernel(x_ref, gain_ref, o_ref, *, batch_major_out):
    x = x_ref[...]
    ms = jnp.mean(x * x, axis=-1, keepdims=True)
    y = (x * lax.rsqrt(ms + NORM_EPS)) * gain_ref[...][None]
    o_ref[...] = jnp.swapaxes(y, 0, 1) if batch_major_out else y


def _batch_major_tiles(t_len, bsz):
    return _time_tile(t_len, bsz) % SUBLANES == 0


def _final(x, gain, batch_major_out):
    t_len, bsz, d = x.shape
    tt = _time_tile(t_len, bsz)
    xspec = pl.BlockSpec((tt, bsz, d), lambda i: (i, 0, 0))
    return pl.pallas_call(
        functools.partial(_final_kernel, batch_major_out=batch_major_out),
        grid=(t_len // tt,),
        in_specs=[xspec, pl.BlockSpec((1, d), lambda i: (0, 0))],
        out_specs=pl.BlockSpec((bsz, tt, d), lambda i: (0, i, 0)) if batch_major_out else xspec,
        out_shape=jax.ShapeDtypeStruct((bsz, t_len, d) if batch_major_out else x.shape, F32),
        compiler_params=_params("parallel"),
        name="final_norm",
    )(x, gain)


def _pad_cols(w):
    n = w.shape[-1]
    return jnp.pad(w, [(0, 0)] * (w.ndim - 1) + [(0, -n % LORA_PAD)])


def _pad_rows(w):
    n = w.shape[-2]
    return jnp.pad(w, [(0, 0)] * (w.ndim - 2) + [(0, -n % LORA_PAD), (0, 0)])


def _trunk(x, mods, lru_h0, lru_conv0, rwkv_shift0, rwkv_wkv0, W):
    new_h, new_conv, new_shift, new_wkv = [], [], [], None
    v_first = None
    fold = _batch_major_tiles(x.shape[1], x.shape[0])
    if not fold:
        x = jnp.transpose(x, (1, 0, 2))
    for layer in range(DEPTH):
        j = layer // 2
        x = _ffn(x, mods, layer, 0, W['ffn_w_in'], W['ffn_w_out'], batch_major_in=fold and layer == 0)
        if layer % 2 == 0:
            x, h_last, buf = _lru(x, mods, layer, lru_h0[j], lru_conv0[j], W['lru_w_in'][j],
                                  W['lru_conv_w'][j], W['lru_conv_b'][j], W['lru_gate_w'][j],
                                  W['lru_gate_b'][j], W['lru_lambda'][j], W['lru_w_out'][j])
            new_h.append(h_last)
            new_conv.append(buf)
        else:
            weights = [W[n][j] for n in ('rwkv_mu', 'rwkv_w_rkv', 'rwkv_w0', 'rwkv_w1', 'rwkv_w2',
                                         'rwkv_a0', 'rwkv_a1', 'rwkv_a2', 'rwkv_g1', 'rwkv_g2',
                                         'rwkv_k_k', 'rwkv_k_a')]
            vres = None if j == 0 else (W['rwkv_v0'][j - 1], W['rwkv_v1'][j - 1], W['rwkv_v2'][j - 1],
                                        v_first)
            r, w, k, v, na, b, g, shift_last = _rwkv_in(x, mods, layer, rwkv_shift0[j], weights, vres)
            if v_first is None:
                v_first = v
            y, new_wkv = _wkv(r, w, k, v, na, b, None if rwkv_wkv0 is None else rwkv_wkv0[j],
                              new_wkv)
            x = _rwkv_out(y, r, k, v, g, x, mods, layer, W['rwkv_ln_w'][j], W['rwkv_ln_b'][j],
                          W['rwkv_r_k'][j], W['rwkv_w_o'][j])
            new_shift.append(shift_last)
        x = _ffn(x, mods, layer, 1, W['ffn_w_in'], W['ffn_w_out'])
    y = _final(x, W['final_gain'], batch_major_out=fold)
    if not fold:
        y = jnp.transpose(y, (1, 0, 2))
    return y, jnp.stack(new_h), jnp.stack(new_conv), jnp.stack(new_shift), new_wkv


def _prep_weights(P):
    row = lambda a: a.reshape(a.shape[0], 1, -1)
    W = {
        'ffn_w_in': P['ffn_w_in'],
        'ffn_w_out': P['ffn_w_out'],
        'lru_w_in': P['lru_w_in'].astype(BF16),
        'lru_conv_w': P['lru_conv_w'],
        'lru_conv_b': row(P['lru_conv_b']),
        'lru_gate_w': P['lru_gate_w'].astype(BF16),
        'lru_gate_b': P['lru_gate_b'],
        'lru_lambda': row(P['lru_lambda']),
        'lru_w_out': P['lru_w_out'].astype(BF16),
        'rwkv_mu': P['rwkv_mu'],
        'rwkv_w_rkv': P['rwkv_w_rkv'].astype(BF16),
        'rwkv_w_o': P['rwkv_w_o'].astype(BF16),
        'rwkv_r_k': row(P['rwkv_r_k']),
        'final_gain': P['final_gain'].reshape(1, -1),
    }
    for n in ('rwkv_w0', 'rwkv_a0', 'rwkv_v0', 'rwkv_k_k', 'rwkv_k_a', 'rwkv_ln_w', 'rwkv_ln_b'):
        W[n] = row(P[n])
    for n in ('rwkv_w1', 'rwkv_a1', 'rwkv_v1', 'rwkv_g1'):
        W[n] = _pad_cols(P[n]).astype(BF16)
    for n in ('rwkv_w2', 'rwkv_a2', 'rwkv_v2', 'rwkv_g2'):
        W[n] = _pad_rows(P[n]).astype(BF16)
    return W


def _run_stream(x, mods, lru_h, lru_conv, rwkv_shift, rwkv_wkv, W):
    y, h, conv, shift, wkv = _trunk(x, mods, lru_h, jnp.transpose(lru_conv, (0, 2, 1, 3)), rwkv_shift,
                                    rwkv_wkv, W)
    return y, h, jnp.transpose(conv, (0, 2, 1, 3)), shift, wkv


def kernel(x_prompt, x_sample, state_lru_h, state_lru_conv, state_rwkv_shift, state_rwkv_wkv, c_prompt, c_sample, ada_w, ada_b, ffn_w_in, ffn_w_out, lru_w_in, lru_conv_w, lru_conv_b, lru_gate_w, lru_gate_b, lru_lambda, lru_w_out, rwkv_mu, rwkv_w_rkv, rwkv_w_o, rwkv_w0, rwkv_w1, rwkv_w2, rwkv_a0, rwkv_a1, rwkv_a2, rwkv_v0, rwkv_v1, rwkv_v2, rwkv_g1, rwkv_g2, rwkv_k_k, rwkv_k_a, rwkv_r_k, rwkv_ln_w, rwkv_ln_b, final_gain):
    P = dict(ffn_w_in=ffn_w_in, ffn_w_out=ffn_w_out,
             lru_w_in=lru_w_in, lru_conv_w=lru_conv_w, lru_conv_b=lru_conv_b, lru_gate_w=lru_gate_w,
             lru_gate_b=lru_gate_b, lru_lambda=lru_lambda, lru_w_out=lru_w_out,
             rwkv_mu=rwkv_mu, rwkv_w_rkv=rwkv_w_rkv, rwkv_w_o=rwkv_w_o, rwkv_w0=rwkv_w0, rwkv_w1=rwkv_w1,
             rwkv_w2=rwkv_w2, rwkv_a0=rwkv_a0, rwkv_a1=rwkv_a1, rwkv_a2=rwkv_a2, rwkv_v0=rwkv_v0,
             rwkv_v1=rwkv_v1, rwkv_v2=rwkv_v2, rwkv_g1=rwkv_g1, rwkv_g2=rwkv_g2, rwkv_k_k=rwkv_k_k,
             rwkv_k_a=rwkv_k_a, rwkv_r_k=rwkv_r_k, rwkv_ln_w=rwkv_ln_w, rwkv_ln_b=rwkv_ln_b,
             final_gain=final_gain)
    W = _prep_weights(P)
    bp, bs = x_prompt.shape[0], x_sample.shape[0]
    mods = _ada(jnp.concatenate([c_prompt, c_sample], axis=0), ada_w, ada_b)
    n_lru, n_rwkv = state_lru_h.shape[0], state_rwkv_shift.shape[0]
    zeros = lambda *s: jnp.zeros(s, F32)
    out_p = _run_stream(x_prompt, mods[:, :bp], zeros(n_lru, bp, D_RNN), zeros(n_lru, bp, CONV_W - 1, D_RNN),
                        zeros(n_rwkv, bp, D_MODEL), None, W)
    out_s = _run_stream(x_sample, mods[:, bp:], state_lru_h, state_lru_conv, state_rwkv_shift,
                        state_rwkv_wkv, W)
    return (out_p[0], out_s[0]) + out_p[1:] + out_s[1:]
```

```python
import functools

import jax
import jax.numpy as jnp
from jax import lax
from jax.experimental import pallas as pl
from jax.experimental.pallas import tpu as pltpu

D_MODEL = 1024
DEPTH = 4
D_RNN = D_MODEL
LRU_BLOCKS = 4
LRU_BLOCK_W = D_RNN // LRU_BLOCKS
CONV_W = 4
RG_C = 8.0
HEAD_SIZE = 64
RWKV_HEADS = D_MODEL // HEAD_SIZE
D_FF = 2816
FFN_RES = 0.5
N_MOD = 9
NORM_EPS = 1e-6
GN_EPS = HEAD_SIZE * 1e-5
DECAY_SCALE = 0.6065306597126334

LANES = 128
SUBLANES = 8
HEAD_PAIRS = D_MODEL // LANES
LORA_PAD = 128
VMEM_LIMIT_BYTES = 56 * 1024 * 1024
ROW_TILE = 512
RWKV_IN_ROW_TILE = 256
FFN_CHUNK = 256
ADA_TILE = 1152
SCAN_CARRY_VREGS = 16
WKV_STATE_COLS = 8 * SUBLANES
WKV_COLS = 12 * SUBLANES
WKV_BATCH_TILE = 8
WKV_TIME_TILE = 32
WKV_CHUNK_UNROLL = 4
WKV_LAG = 3

F32 = jnp.float32
BF16 = jnp.bfloat16


def _dot(a, b):
    return jnp.dot(a, b, preferred_element_type=F32)


def _params(*sem):
    return pltpu.CompilerParams(dimension_semantics=sem, vmem_limit_bytes=VMEM_LIMIT_BYTES)


def _resident(block_shape, index_map):
    return pl.BlockSpec(block_shape, index_map, pipeline_mode=pl.Buffered(1))


def _rms_mod(x, shift, scale):
    ms = jnp.mean(x * x, axis=-1, keepdims=True)
    return (x * lax.rsqrt(ms + NORM_EPS)) * (1.0 + scale)[None] + shift[None]


def _softplus(x):
    return jnp.maximum(x, 0.0) + jnp.log1p(jnp.exp(-jnp.abs(x)))


def _head_ones():
    r = lax.broadcasted_iota(jnp.int32, (LANES, LANES), 0) // HEAD_SIZE
    c = lax.broadcasted_iota(jnp.int32, (LANES, LANES), 1) // HEAD_SIZE
    return (r == c).astype(BF16)


def _head_sum(x, ones, two_pass=True):
    outs = []
    for c in range(HEAD_PAIRS):
        xc = x[:, c * LANES:(c + 1) * LANES]
        hi = xc.astype(BF16)
        s = _dot(hi, ones)
        if two_pass:
            s = s + _dot((xc - hi.astype(F32)).astype(BF16), ones)
        outs.append(s)
    return jnp.concatenate(outs, axis=-1)


def _head_dot(x):
    head0 = lax.broadcasted_iota(jnp.int32, x.shape, 1) < HEAD_SIZE
    s0 = jnp.sum(jnp.where(head0, x, 0.0), axis=1, keepdims=True)
    s1 = jnp.sum(jnp.where(head0, 0.0, x), axis=1, keepdims=True)
    return jnp.where(head0, s0, s1)


def _ada_kernel(c_ref, w_ref, b_ref, o_ref):
    c = c_ref[...]
    s = (c * jax.nn.sigmoid(c)).astype(BF16)
    o_ref[...] = _dot(s, w_ref[...].astype(BF16)) + b_ref[...]


def _ada(c_all, ada_w, ada_b):
    n = c_all.shape[0]
    nd = N_MOD * D_MODEL
    return pl.pallas_call(
        _ada_kernel,
        grid=(DEPTH, nd // ADA_TILE),
        in_specs=[
            pl.BlockSpec((n, D_MODEL), lambda l, j: (0, 0)),
            pl.BlockSpec((None, D_MODEL, ADA_TILE), lambda l, j: (l, 0, j)),
            pl.BlockSpec((None, 1, ADA_TILE), lambda l, j: (l, 0, j)),
        ],
        out_specs=pl.BlockSpec((None, n, ADA_TILE), lambda l, j: (l, 0, j)),
        out_shape=jax.ShapeDtypeStruct((DEPTH, n, nd), F32),
        compiler_params=_params("parallel", "parallel"),
        name="ada_mod",
    )(c_all, ada_w, ada_b.reshape(DEPTH, 1, nd))


def _mod_spec(bsz, layer, m):
    return pl.BlockSpec((None, bsz, D_MODEL), lambda *_: (layer, 0, m))


def _time_tile(t_len, bsz, rows=ROW_TILE):
    return max(1, min(t_len, rows // bsz))


def _ffn_kernel(x_ref, sh_ref, sc_ref, g_ref, win_ref, wout_ref, o_ref, *, batch_major_in):
    x = x_ref[...]
    if batch_major_in:
        x = jnp.swapaxes(x, 0, 1)
    tt, bsz, d = x.shape
    h = _rms_mod(x, sh_ref[...], sc_ref[...]).reshape(tt * bsz, d).astype(BF16)
    acc = jnp.zeros((tt * bsz, d), F32)
    for j in range(D_FF // FFN_CHUNK):
        lo = j * FFN_CHUNK
        gate = _dot(h, win_ref[:, lo:lo + FFN_CHUNK].astype(BF16))
        up = _dot(h, win_ref[:, D_FF + lo:D_FF + lo + FFN_CHUNK].astype(BF16))
        act = (gate * jax.nn.sigmoid(gate) * up).astype(BF16)
        acc = acc + _dot(act, wout_ref[lo:lo + FFN_CHUNK, :].astype(BF16))
    o_ref[...] = x + (FFN_RES * (1.0 + g_ref[...]))[None] * acc.reshape(tt, bsz, d)


def _ffn(x, mods, layer, sub, w_in, w_out, batch_major_in=False):
    if batch_major_in:
        bsz, t_len, d = x.shape
    else:
        t_len, bsz, d = x.shape
    tt = _time_tile(t_len, bsz)
    m0 = 0 if sub == 0 else 6
    xspec = pl.BlockSpec((tt, bsz, d), lambda i: (i, 0, 0))
    return pl.pallas_call(
        functools.partial(_ffn_kernel, batch_major_in=batch_major_in),
        grid=(t_len // tt,),
        in_specs=[
            pl.BlockSpec((bsz, tt, d), lambda i: (0, i, 0)) if batch_major_in else xspec,
            _mod_spec(bsz, layer, m0), _mod_spec(bsz, layer, m0 + 1), _mod_spec(bsz, layer, m0 + 2),
            _resident((None, None, d, 2 * D_FF), lambda i: (layer, sub, 0, 0)),
            _resident((None, None, D_FF, d), lambda i: (layer, sub, 0, 0)),
        ],
        out_specs=xspec,
        out_shape=jax.ShapeDtypeStruct((t_len, bsz, d), F32),
        compiler_params=_params("parallel"),
        name="ffn",
    )(x, mods, mods, mods, w_in, w_out)


def _lru_kernel(x_ref, sh_ref, sc_ref, g_ref, h0_ref, conv0_ref, win_ref, cw_ref, cb_ref, gw_ref,
                gb_ref, lam_ref, wout_ref, o_ref, hlast_ref, buf_ref, a_scr, u_scr):
    @pl.when(pl.program_id(0) == 0)
    def _():
        hlast_ref[...] = h0_ref[...]
        buf_ref[...] = conv0_ref[...]

    x = x_ref[...]
    tt, bsz, d = x.shape
    tm = tt * bsz
    h = _rms_mod(x, sh_ref[...], sc_ref[...]).reshape(tm, d).astype(BF16)
    proj = _dot(h, win_ref[...])
    gate_branch = proj[:, :d]
    xp = jnp.concatenate([buf_ref[...], proj[:, d:].reshape(tt, bsz, d)], axis=0)
    cw = cw_ref[...]
    xc = cb_ref[...][None]
    for j in range(CONV_W):
        xc = xc + xp[j:j + tt] * cw[j:j + 1][None]
    buf_ref[...] = xp[tt:tt + CONV_W - 1]
    xc = xc.reshape(tm, d)

    lam = lam_ref[...]
    log_sig_lam = -_softplus(-lam)
    for n in range(LRU_BLOCKS):
        sl = slice(n * LRU_BLOCK_W, (n + 1) * LRU_BLOCK_W)
        xb = xc[:, sl]
        gates = _dot(xb.astype(BF16), gw_ref[n]) + gb_ref[n:n + 1, :]
        r = jax.nn.sigmoid(gates[:, :LRU_BLOCK_W])
        ig = jax.nn.sigmoid(gates[:, LRU_BLOCK_W:])
        log_a = RG_C * r * log_sig_lam[:, sl]
        a_scr[:, sl] = jnp.exp(log_a)
        th = jnp.tanh(log_a)
        u_scr[:, sl] = jnp.sqrt(-2.0 * th / (1.0 - th)) * (ig * xb)

    cwid = min(d, SCAN_CARRY_VREGS * SUBLANES * LANES // bsz)
    for c in range(d // cwid):
        ls = slice(c * cwid, (c + 1) * cwid)

        def step(t, hc, ls=ls):
            rows = pl.ds(pl.multiple_of(t * bsz, bsz), bsz)
            hc = a_scr[rows, ls] * hc + u_scr[rows, ls]
            u_scr[rows, ls] = hc
            return hc

        hlast_ref[:, ls] = lax.fori_loop(0, tt, step, hlast_ref[:, ls])

    gelu = 0.5 * gate_branch * (1.0 + jnp.tanh(
        0.7978845608028654 * (gate_branch + 0.044715 * (gate_branch * gate_branch * gate_branch))))
    y = _dot((u_scr[...] * gelu).astype(BF16), wout_ref[...])
    o_ref[...] = x + (1.0 + g_ref[...])[None] * y.reshape(tt, bsz, d)


def _lru(x, mods, layer, h0, conv0, w_in, conv_w, conv_b, gate_w, gate_b, lam, w_out):
    t_len, bsz, d = x.shape
    tt = _time_tile(t_len, bsz)
    xspec = pl.BlockSpec((tt, bsz, d), lambda i: (i, 0, 0))

    def full(a):
        return _resident(a.shape, lambda i, n=a.ndim: (0,) * n)

    args = (h0, conv0, w_in, conv_w, conv_b, gate_w, gate_b, lam, w_out)
    return pl.pallas_call(
        _lru_kernel,
        grid=(t_len // tt,),
        in_specs=[xspec, _mod_spec(bsz, layer, 3), _mod_spec(bsz, layer, 4), _mod_spec(bsz, layer, 5)]
        + [full(a) for a in args],
        out_specs=[xspec,
                   pl.BlockSpec((bsz, d), lambda i: (0, 0)),
                   pl.BlockSpec((CONV_W - 1, bsz, d), lambda i: (0, 0, 0))],
        out_shape=[jax.ShapeDtypeStruct(x.shape, F32),
                   jax.ShapeDtypeStruct((bsz, d), F32),
                   jax.ShapeDtypeStruct((CONV_W - 1, bsz, d), F32)],
        scratch_shapes=[pltpu.VMEM((tt * bsz, d), F32), pltpu.VMEM((tt * bsz, d), F32)],
        compiler_params=_params("arbitrary"),
        name="rglru",
    )(x, mods, mods, mods, *args)


def _rwkv_in_kernel(*refs, has_vres):
    (x_ref, sh_ref, sc_ref, shift0_ref, mu_ref, wrkv_ref, w0_ref, w1_ref, w2_ref, a0_ref, a1_ref,
     a2_ref, g1_ref, g2_ref, kk_ref, ka_ref) = refs[:16]
    refs = refs[16:]
    if has_vres:
        v0_ref, v1_ref, v2_ref, vfirst_ref = refs[:4]
        refs = refs[4:]
    r_out, w_out, k_out, v_out, na_out, b_out, g_out, shift_ref = refs

    @pl.when(pl.program_id(0) == 0)
    def _():
        shift_ref[...] = shift0_ref[...]

    x = x_ref[...]
    tt, bsz, d = x.shape
    tm = tt * bsz
    xf = _rms_mod(x, sh_ref[...], sc_ref[...])
    prev = jnp.concatenate([shift_ref[...][None], xf[:tt - 1]], axis=0)
    shift_ref[...] = xf[tt - 1]
    xf_b = xf.reshape(tm, d).astype(BF16)
    dx_b = (prev - xf).reshape(tm, d).astype(BF16)

    def mix(j):
        return xf_b + dx_b * mu_ref[j:j + 1, :].astype(BF16)

    r = _dot(mix(0), wrkv_ref[0])
    k = _dot(mix(1), wrkv_ref[1])
    xm_v = mix(2)
    v = _dot(xm_v, wrkv_ref[2])
    z = w0_ref[...] + _dot(jnp.tanh(_dot(mix(3), w1_ref[...])).astype(BF16), w2_ref[...])
    decay = jnp.exp(-DECAY_SCALE * jax.nn.sigmoid(z))
    if has_vres:
        mix_v = jax.nn.sigmoid(v0_ref[...] + _dot(_dot(xm_v, v1_ref[...]).astype(BF16), v2_ref[...]))
        v = v + (vfirst_ref[...].reshape(tm, d) - v) * mix_v
    a = jax.nn.sigmoid(a0_ref[...] + _dot(_dot(mix(4), a1_ref[...]).astype(BF16), a2_ref[...]))
    g = _dot(jax.nn.sigmoid(_dot(mix(5), g1_ref[...])).astype(BF16), g2_ref[...])

    shp = (tt, bsz, d)
    r_out[...] = r.reshape(shp)
    w_out[...] = decay.reshape(shp)
    k_out[...] = (k * (1.0 + (a - 1.0) * ka_ref[...])).reshape(shp)
    v_out[...] = v.reshape(shp)
    kk = k * kk_ref[...]
    norm = jnp.concatenate([jnp.sqrt(_head_dot(kk[:, c * LANES:(c + 1) * LANES] ** 2))
                            for c in range(HEAD_PAIRS)], axis=1)
    kk = kk / jnp.maximum(norm, 1e-12)
    na_out[...] = (-kk).reshape(shp)
    b_out[...] = (kk * a).reshape(shp)
    g_out[...] = g.reshape(shp)


def _rwkv_in(x, mods, layer, shift0, weights, vres):
    t_len, bsz, d = x.shape
    tt = _time_tile(t_len, bsz, RWKV_IN_ROW_TILE)
    xspec = pl.BlockSpec((tt, bsz, d), lambda i: (i, 0, 0))

    def full(a):
        return _resident(a.shape, lambda i, n=a.ndim: (0,) * n)

    args = [shift0] + list(weights)
    in_specs = [xspec, _mod_spec(bsz, layer, 3), _mod_spec(bsz, layer, 4)] + [full(a) for a in args]
    if vres is not None:
        v0, v1, v2, v_first = vres
        args += [v0, v1, v2, v_first]
        in_specs += [full(v0), full(v1), full(v2), xspec]
    act = jax.ShapeDtypeStruct(x.shape, F32)
    return pl.pallas_call(
        functools.partial(_rwkv_in_kernel, has_vres=vres is not None),
        grid=(t_len // tt,),
        in_specs=in_specs,
        out_specs=[xspec] * 7 + [pl.BlockSpec((bsz, d), lambda i: (0, 0))],
        out_shape=[act] * 7 + [jax.ShapeDtypeStruct((bsz, d), F32)],
        compiler_params=_params("arbitrary"),
        name="rwkv_in",
    )(x, mods, mods, *args)


def _wkv_kernel(*refs, has_state, n_prev):
    r_ref, w_ref, k_ref, v_ref, a_ref, b_ref = refs[:6]
    refs = list(refs[6:])
    s0_ref = refs.pop(0) if has_state else None
    prev_ref = refs.pop(0) if n_prev else None
    y_ref, s_out_ref, s_scr = refs
    tb, bb = r_ref.shape[0], r_ref.shape[1]

    @pl.when(pl.program_id(1) == 0)
    def _():
        if has_state:
            def pack(b, carry):
                for p in range(HEAD_PAIRS):
                    s_scr[b, p] = jnp.concatenate([s0_ref[b, 2 * p], s0_ref[b, 2 * p + 1]], axis=1)
                return carry
            lax.fori_loop(0, bb, pack, 0)
        else:
            s_scr[...] = jnp.zeros_like(s_scr)

    s_ref = s_scr
    head0 = lax.broadcasted_iota(jnp.int32, (SUBLANES, LANES), 1) < HEAD_SIZE
    rows_s = lax.broadcasted_iota(jnp.int32, (WKV_STATE_COLS, LANES), 0)
    rows_y = lax.broadcasted_iota(jnp.int32, (WKV_COLS, 2 * LANES), 0)
    seq_half = lax.broadcasted_iota(jnp.int32, (WKV_COLS, 2 * LANES), 1) // LANES
    zero8 = jnp.zeros((SUBLANES, LANES), F32)
    zero_rows = jnp.zeros((WKV_STATE_COLS, HEAD_SIZE), F32)
    contract_lanes = (((1,), (1,)), ((), ()))

    def split(x):
        return [jnp.where(head0, x, 0.0), jnp.where(head0, 0.0, x)]

    head_dot = _head_dot

    def halves(x):
        return [x[:, :HEAD_SIZE], x[:, HEAD_SIZE:]]

    def load(ref, t, b0, p):
        return ref[t, pl.ds(b0, SUBLANES), pl.ds(p * LANES, LANES)]

    def stage_a(c, b0, p):
        t1, t2 = 2 * c, 2 * c + 1
        w1 = load(w_ref, t1, b0, p)
        vectors = [load(a_ref, t1, b0, p), load(a_ref, t2, b0, p) * w1, w1 * load(r_ref, t1, b0, p),
                   w1 * load(w_ref, t2, b0, p) * load(r_ref, t2, b0, p)]
        lmat = jnp.concatenate([h for x in vectors for h in split(x)], axis=0).astype(BF16)
        none = jnp.zeros_like(lmat)
        proj = None
        for i in range(SUBLANES):
            l_i = jnp.where(rows_s % SUBLANES == i, lmat, none)
            d = lax.dot_general(s_ref[b0 + i, p].astype(BF16), l_i, contract_lanes,
                                preferred_element_type=F32)
            proj = d if proj is None else proj + d
        return proj

    def stage_b(c, b0, p, proj):
        t1, t2 = 2 * c, 2 * c + 1
        r1, w1, k1, v1, b1 = (load(ref, t1, b0, p) for ref in (r_ref, w_ref, k_ref, v_ref, b_ref))
        r2, w2, k2, v2, a2, b2 = (load(ref, t2, b0, p) for ref in (r_ref, w_ref, k_ref, v_ref, a_ref, b_ref))
        w12 = w1 * w2
        b1a2 = head_dot(b1 * a2)
        k1a2 = head_dot(k1 * a2)
        ymat = jnp.concatenate(split(b1 * w2 + b1a2 * b2) + split(b2) + [zero8] * 4
                               + split(k1 * w2 + k1a2 * b2) + split(k2), axis=0).astype(BF16)
        ymat = jnp.concatenate([ymat, ymat], axis=1)
        none = jnp.zeros_like(ymat)
        v_rows = jnp.concatenate([zero_rows] + halves(v1) + halves(v2), axis=0)
        pb = jnp.concatenate([proj, v_rows.T[:, WKV_STATE_COLS:]], axis=1).astype(BF16)
        for i in range(0, SUBLANES, 2):
            upd = _dot(pb, jnp.where(rows_y % SUBLANES == i + seq_half, ymat, none))
            s_ref[b0 + i, p] = s_ref[b0 + i, p] * w12[i:i + 1] + upd[:, :LANES]
            s_ref[b0 + i + 1, p] = s_ref[b0 + i + 1, p] * w12[i + 1:i + 2] + upd[:, LANES:]
        pt = proj.T

        def rows_of(j):
            lo = 2 * j * SUBLANES
            return jnp.concatenate([pt[lo:lo + SUBLANES], pt[lo + SUBLANES:lo + 2 * SUBLANES]], axis=1)

        u1, s0a2, s0r1, s0r2 = rows_of(0), rows_of(1), rows_of(2), rows_of(3)
        b2r2 = head_dot(b2 * r2)
        w2r2 = w2 * r2
        y_ref[t1, pl.ds(b0, SUBLANES), pl.ds(p * LANES, LANES)] = (
            s0r1 + u1 * head_dot(b1 * r1) + v1 * head_dot(k1 * r1))
        y_ref[t2, pl.ds(b0, SUBLANES), pl.ds(p * LANES, LANES)] = (
            s0r2 + u1 * (head_dot(b1 * w2r2) + b1a2 * b2r2) + v1 * (head_dot(k1 * w2r2) + k1a2 * b2r2)
            + s0a2 * b2r2 + v2 * head_dot(k2 * r2))

    tiles = [(g * SUBLANES, p) for g in range(bb // SUBLANES) for p in range(HEAD_PAIRS)]
    n_chunks = tb // 2
    unroll = max(u for u in range(1, WKV_CHUNK_UNROLL + 1) if n_chunks % u == 0)

    def t_body(cu, carry):
        work = [(cu * unroll + u, b0, p) for u in range(unroll) for (b0, p) in tiles]
        pending = {}
        for step in range(len(work) + WKV_LAG):
            if step < len(work):
                pending[step] = stage_a(*work[step])
            if step >= WKV_LAG:
                stage_b(*work[step - WKV_LAG], pending.pop(step - WKV_LAG))
        return carry

    lax.fori_loop(0, n_chunks // unroll, t_body, 0)

    @pl.when(pl.program_id(1) == pl.num_programs(1) - 1)
    def _():
        def unpack(b, carry):
            for n in range(n_prev):
                s_out_ref[n, b] = prev_ref[n, b]
            for p in range(HEAD_PAIRS):
                s = s_scr[b, p]
                s_out_ref[n_prev, b, 2 * p] = s[:, :HEAD_SIZE]
                s_out_ref[n_prev, b, 2 * p + 1] = s[:, HEAD_SIZE:]
            return carry
        lax.fori_loop(0, bb, unpack, 0)


def _wkv(r, w, k, v, a, b, s0, prev_states):
    t_len, bsz, d = r.shape
    bb = min(bsz, WKV_BATCH_TILE)
    tb = min(t_len, WKV_TIME_TILE)
    assert tb % 2 == 0 and t_len % tb == 0 and bsz % bb == 0
    state = (RWKV_HEADS, HEAD_SIZE, HEAD_SIZE)
    xspec = pl.BlockSpec((tb, bb, d), lambda i, j: (j, i, 0))
    has_state = s0 is not None
    n_prev = 0 if prev_states is None else prev_states.shape[0]
    extra, extra_specs = [], []
    if has_state:
        extra.append(s0)
        extra_specs.append(pl.BlockSpec((bb,) + state, lambda i, j: (i, 0, 0, 0)))
    if n_prev:
        extra.append(prev_states)
        extra_specs.append(pl.BlockSpec((n_prev, bb) + state, lambda i, j: (0, i, 0, 0, 0)))
    return pl.pallas_call(
        functools.partial(_wkv_kernel, has_state=has_state, n_prev=n_prev),
        grid=(bsz // bb, t_len // tb),
        in_specs=[xspec] * 6 + extra_specs,
        out_specs=[xspec, pl.BlockSpec((n_prev + 1, bb) + state, lambda i, j: (0, i, 0, 0, 0))],
        out_shape=[jax.ShapeDtypeStruct(r.shape, F32),
                   jax.ShapeDtypeStruct((n_prev + 1, bsz) + state, F32)],
        scratch_shapes=[pltpu.VMEM((bb, HEAD_PAIRS, HEAD_SIZE, LANES), F32)],
        compiler_params=_params("parallel", "arbitrary"),
        name="wkv",
    )(r, w, k, v, a, b, *extra)


def _rwkv_out_kernel(y_ref, r_ref, k_ref, v_ref, g_ref, x_ref, gate_ref, lnw_ref, lnb_ref, rk_ref,
                     wo_ref, o_ref):
    x = x_ref[...]
    tt, bsz, d = x.shape
    tm = tt * bsz
    ones = _head_ones()
    y = y_ref[...].reshape(tm, d)
    yc = y - _head_sum(y, ones) * (1.0 / HEAD_SIZE)
    var = _head_sum(yc * yc, ones) * (1.0 / HEAD_SIZE)
    yn = yc * lax.rsqrt(var + GN_EPS) * lnw_ref[...] + lnb_ref[...]
    rk = r_ref[...].reshape(tm, d) * k_ref[...].reshape(tm, d) * rk_ref[...]
    bonus = _head_sum(rk, ones, two_pass=False) * v_ref[...].reshape(tm, d)
    out = _dot(((yn + bonus) * g_ref[...].reshape(tm, d)).astype(BF16), wo_ref[...])
    o_ref[...] = x + (1.0 + gate_ref[...])[None] * out.reshape(tt, bsz, d)


def _rwkv_out(y, r, k, v, g, x, mods, layer, ln_w, ln_b, r_k, w_o):
    t_len, bsz, d = x.shape
    tt = _time_tile(t_len, bsz)
    xspec = pl.BlockSpec((tt, bsz, d), lambda i: (i, 0, 0))

    def full(a):
        return _resident(a.shape, lambda i, n=a.ndim: (0,) * n)

    return pl.pallas_call(
        _rwkv_out_kernel,
        grid=(t_len // tt,),
        in_specs=[xspec] * 6 + [_mod_spec(bsz, layer, 5)] + [full(a) for a in (ln_w, ln_b, r_k, w_o)],
        out_specs=xspec,
        out_shape=jax.ShapeDtypeStruct(x.shape, F32),
        compiler_params=_params("parallel"),
        name="rwkv_out",
    )(y, r, k, v, g, x, mods, ln_w, ln_b, r_k, w_o)


def _final_kernel(x_ref, gain_ref, o_ref, *, batch_major_out):
    x = x_ref[...]
    ms = jnp.mean(x * x, axis=-1, keepdims=True)
    y = (x * lax.rsqrt(ms + NORM_EPS)) * gain_ref[...][None]
    o_ref[...] = jnp.swapaxes(y, 0, 1) if batch_major_out else y


def _batch_major_tiles(t_len, bsz):
    return _time_tile(t_len, bsz) % SUBLANES == 0


def _final(x, gain, batch_major_out):
    t_len, bsz, d = x.shape
    tt = _time_tile(t_len, bsz)
    xspec = pl.BlockSpec((tt, bsz, d), lambda i: (i, 0, 0))
    return pl.pallas_call(
        functools.partial(_final_kernel, batch_major_out=batch_major_out),
        grid=(t_len // tt,),
        in_specs=[xspec, pl.BlockSpec((1, d), lambda i: (0, 0))],
        out_specs=pl.BlockSpec((bsz, tt, d), lambda i: (0, i, 0)) if batch_major_out else xspec,
        out_shape=jax.ShapeDtypeStruct((bsz, t_len, d) if batch_major_out else x.shape, F32),
        compiler_params=_params("parallel"),
        name="final_norm",
    )(x, gain)


def _pad_cols(w):
    n = w.shape[-1]
    return jnp.pad(w, [(0, 0)] * (w.ndim - 1) + [(0, -n % LORA_PAD)])


def _pad_rows(w):
    n = w.shape[-2]
    return jnp.pad(w, [(0, 0)] * (w.ndim - 2) + [(0, -n % LORA_PAD), (0, 0)])


def _trunk(x, mods, lru_h0, lru_conv0, rwkv_shift0, rwkv_wkv0, W):
    new_h, new_conv, new_shift, new_wkv = [], [], [], None
    v_first = None
    fold = _batch_major_tiles(x.shape[1], x.shape[0])
    if not fold:
        x = jnp.transpose(x, (1, 0, 2))
    for layer in range(DEPTH):
        j = layer // 2
        x = _ffn(x, mods, layer, 0, W['ffn_w_in'], W['ffn_w_out'], batch_major_in=fold and layer == 0)
        if layer % 2 == 0:
            x, h_last, buf = _lru(x, mods, layer, lru_h0[j], lru_conv0[j], W['lru_w_in'][j],
                                  W['lru_conv_w'][j], W['lru_conv_b'][j], W['lru_gate_w'][j],
                                  W['lru_gate_b'][j], W['lru_lambda'][j], W['lru_w_out'][j])
            new_h.append(h_last)
            new_conv.append(buf)
        else:
            weights = [W[n][j] for n in ('rwkv_mu', 'rwkv_w_rkv', 'rwkv_w0', 'rwkv_w1', 'rwkv_w2',
                                         'rwkv_a0', 'rwkv_a1', 'rwkv_a2', 'rwkv_g1', 'rwkv_g2',
                                         'rwkv_k_k', 'rwkv_k_a')]
            vres = None if j == 0 else (W['rwkv_v0'][j - 1], W['rwkv_v1'][j - 1], W['rwkv_v2'][j - 1],
                                        v_first)
            r, w, k, v, na, b, g, shift_last = _rwkv_in(x, mods, layer, rwkv_shift0[j], weights, vres)
            if v_first is None:
                v_first = v
            y, new_wkv = _wkv(r, w, k, v, na, b, None if rwkv_wkv0 is None else rwkv_wkv0[j],
                              new_wkv)
            x = _rwkv_out(y, r, k, v, g, x, mods, layer, W['rwkv_ln_w'][j], W['rwkv_ln_b'][j],
                          W['rwkv_r_k'][j], W['rwkv_w_o'][j])
            new_shift.append(shift_last)
        x = _ffn(x, mods, layer, 1, W['ffn_w_in'], W['ffn_w_out'])
    y = _final(x, W['final_gain'], batch_major_out=fold)
    if not fold:
        y = jnp.transpose(y, (1, 0, 2))
    return y, jnp.stack(new_h), jnp.stack(new_conv), jnp.stack(new_shift), new_wkv


def _prep_weights(P):
    row = lambda a: a.reshape(a.shape[0], 1, -1)
    W = {
        'ffn_w_in': P['ffn_w_in'],
        'ffn_w_out': P['ffn_w_out'],
        'lru_w_in': P['lru_w_in'].astype(BF16),
        'lru_conv_w': P['lru_conv_w'],
        'lru_conv_b': row(P['lru_conv_b']),
        'lru_gate_w': P['lru_gate_w'].astype(BF16),
        'lru_gate_b': P['lru_gate_b'],
        'lru_lambda': row(P['lru_lambda']),
        'lru_w_out': P['lru_w_out'].astype(BF16),
        'rwkv_mu': P['rwkv_mu'],
        'rwkv_w_rkv': P['rwkv_w_rkv'].astype(BF16),
        'rwkv_w_o': P['rwkv_w_o'].astype(BF16),
        'rwkv_r_k': row(P['rwkv_r_k']),
        'final_gain': P['final_gain'].reshape(1, -1),
    }
    for n in ('rwkv_w0', 'rwkv_a0', 'rwkv_v0', 'rwkv_k_k', 'rwkv_k_a', 'rwkv_ln_w', 'rwkv_ln_b'):
        W[n] = row(P[n])
    for n in ('rwkv_w1', 'rwkv_a1', 'rwkv_v1', 'rwkv_g1'):
        W[n] = _pad_cols(P[n]).astype(BF16)
    for n in ('rwkv_w2', 'rwkv_a2', 'rwkv_v2', 'rwkv_g2'):
        W[n] = _pad_rows(P[n]).astype(BF16)
    return W


def _run_stream(x, mods, lru_h, lru_conv, rwkv_shift, rwkv_wkv, W):
    y, h, conv, shift, wkv = _trunk(x, mods, lru_h, jnp.transpose(lru_conv, (0, 2, 1, 3)), rwkv_shift,
                                    rwkv_wkv, W)
    return y, h, jnp.transpose(conv, (0, 2, 1, 3)), shift, wkv


def kernel(x_prompt, x_sample, state_lru_h, state_lru_conv, state_rwkv_shift, state_rwkv_wkv, c_prompt, c_sample, ada_w, ada_b, ffn_w_in, ffn_w_out, lru_w_in, lru_conv_w, lru_conv_b, lru_gate_w, lru_gate_b, lru_lambda, lru_w_out, rwkv_mu, rwkv_w_rkv, rwkv_w_o, rwkv_w0, rwkv_w1, rwkv_w2, rwkv_a0, rwkv_a1, rwkv_a2, rwkv_v0, rwkv_v1, rwkv_v2, rwkv_g1, rwkv_g2, rwkv_k_k, rwkv_k_a, rwkv_r_k, rwkv_ln_w, rwkv_ln_b, final_gain):
    P = dict(ffn_w_in=ffn_w_in, ffn_w_out=ffn_w_out,
             lru_w_in=lru_w_in, lru_conv_w=lru_conv_w, lru_conv_b=lru_conv_b, lru_gate_w=lru_gate_w,
             lru_gate_b=lru_gate_b, lru_lambda=lru_lambda, lru_w_out=lru_w_out,
             rwkv_mu=rwkv_mu, rwkv_w_rkv=rwkv_w_rkv, rwkv_w_o=rwkv_w_o, rwkv_w0=rwkv_w0, rwkv_w1=rwkv_w1,
             rwkv_w2=rwkv_w2, rwkv_a0=rwkv_a0, rwkv_a1=rwkv_a1, rwkv_a2=rwkv_a2, rwkv_v0=rwkv_v0,
             rwkv_v1=rwkv_v1, rwkv_v2=rwkv_v2, rwkv_g1=rwkv_g1, rwkv_g2=rwkv_g2, rwkv_k_k=rwkv_k_k,
             rwkv_k_a=rwkv_k_a, rwkv_r_k=rwkv_r_k, rwkv_ln_w=rwkv_ln_w, rwkv_ln_b=rwkv_ln_b,
             final_gain=final_gain)
    W = _prep_weights(P)
    bp, bs = x_prompt.shape[0], x_sample.shape[0]
    mods = _ada(jnp.concatenate([c_prompt, c_sample], axis=0), ada_w, ada_b)
    n_lru, n_rwkv = state_lru_h.shape[0], state_rwkv_shift.shape[0]
    zeros = lambda *s: jnp.zeros(s, F32)
    out_p = _run_stream(x_prompt, mods[:, :bp], zeros(n_lru, bp, D_RNN), zeros(n_lru, bp, CONV_W - 1, D_RNN),
                        zeros(n_rwkv, bp, D_MODEL), None, W)
    out_s = _run_stream(x_sample, mods[:, bp:], state_lru_h, state_lru_conv, state_rwkv_shift,
                        state_rwkv_wkv, W)
    return (out_p[0], out_s[0]) + out_p[1:] + out_s[1:]
```
